```python
import math
import numpy as np
import jax
import jax.numpy as jnp
from jax import lax

D_MODEL = 1024
BATCH = 1
SEQ = 16384
DEPTH = 1

ATTN_PATTERNS = ((128, 1), (512, 4), (2048, 16))
N_GROUPS_A = 3
HEADS_PER_GROUP = 4
N_HEADS_A = N_GROUPS_A * HEADS_PER_GROUP
HEAD_DIM_A = 128
D_ATTN = N_HEADS_A * HEAD_DIM_A
D_ATTN_OUT = HEADS_PER_GROUP * HEAD_DIM_A
BLOCK_A = 128
N_HEADS_B = 8
HEAD_DIM_K = 128
HEAD_DIM_V = 128
D_KEY_B = N_HEADS_B * HEAD_DIM_K
D_VAL_B = N_HEADS_B * HEAD_DIM_V
CONV_WIDTH = 4
CHUNK = 64
N_BRANCHES = 2
D_FF = ((8 * D_MODEL // 3 + 255) // 256) * 256
EPS = 1e-6
IN_SPLITS = (D_ATTN, D_ATTN, D_ATTN, D_KEY_B, D_KEY_B, D_VAL_B, D_VAL_B, N_HEADS_B, N_HEADS_B, N_BRANCHES * D_MODEL)
D_IN = 3 * D_ATTN + 2 * D_KEY_B + 2 * D_VAL_B + 2 * N_HEADS_B + N_BRANCHES * D_MODEL

kernel_name = 'hybrid_dilated_swa_gated_deltanet'


def rmsnorm(x, w):
    xf = x.astype(jnp.float32)
    xf = xf * lax.rsqrt(jnp.mean(xf * xf, axis=-1, keepdims=True) + EPS)
    return xf.astype(x.dtype) * w


def l2norm(t):
    return t * lax.rsqrt(jnp.sum(t * t, axis=-1, keepdims=True) + EPS)


def alibi_slopes():
    return jnp.exp2(-8.0 * jnp.arange(1, N_HEADS_A + 1, dtype=jnp.float32) / N_HEADS_A)


def dilated_window_attention(q, k, v, slopes, window, dilation):
    b, s, h, dh = q.shape
    f32 = jnp.float32
    w_sub = window // dilation
    span = BLOCK_A * dilation
    s_pad = -(-s // span) * span
    nb = s_pad // span

    def to_blocks(t):
        t = jnp.pad(t, ((0, 0), (0, s_pad - s), (0, 0), (0, 0)))
        return t.reshape(b, nb, BLOCK_A, dilation, h, dh)

    qb, kb, vb = to_blocks(q), to_blocks(k), to_blocks(v)

    def with_prev(t):
        prev = jnp.pad(t[:, :-1], ((0, 0), (1, 0), (0, 0), (0, 0), (0, 0), (0, 0)))
        return jnp.concatenate([prev, t], axis=2)

    kc, vc = with_prev(kb), with_prev(vb)
    scores = jnp.einsum('bnirhd,bnjrhd->bnrhij', qb, kc).astype(f32) * (dh ** -0.5)
    i = jnp.arange(BLOCK_A)[:, None]
    j = jnp.arange(2 * BLOCK_A)[None, :]
    delta = BLOCK_A + i - j
    band = (delta >= 0) & (delta <= w_sub)
    has_prev = (jnp.arange(nb)[:, None, None] > 0) | (j >= BLOCK_A)[None]
    valid = band[None] & has_prev
    bias = -(slopes.astype(f32) * dilation)[:, None, None] * delta.astype(f32)
    scores = jnp.where(valid[None, :, None, None], scores + bias, -jnp.inf)
    m = jnp.max(scores, axis=-1, keepdims=True)
    p = jnp.exp(scores - m)
    den = jnp.sum(p, axis=-1)
    o = jnp.einsum('bnrhij,bnjrhd->bnirhd', p, vc.astype(f32))
    o = o / jnp.transpose(den, (0, 1, 4, 2, 3))[..., None]
    lse = jnp.transpose(m[..., 0] + jnp.log(den), (0, 1, 4, 2, 3))
    o = o.reshape(b, s_pad, h, dh)[:, :s]
    lse = lse.reshape(b, s_pad, h)[:, :s]
    return o, lse


def causal_depthwise_conv(x, w):
    c = x.shape[-1]
    return lax.conv_general_dilated(x, w[:, None, :].astype(x.dtype), window_strides=(1,),
                                    padding=((CONV_WIDTH - 1, 0),),
                                    dimension_numbers=('NWC', 'WIO', 'NWC'),
                                    feature_group_count=c)


def gated_delta_rule(q, k, v, g, beta):
    b, s, h, dk = q.shape
    dv = v.shape[-1]
    n = s // CHUNK

    def chunks(t):
        return jnp.moveaxis(t.reshape(b, n, CHUNK, h, -1), 3, 1)

    q, k, v = chunks(q), chunks(k), chunks(v)
    g = jnp.moveaxis(g.reshape(b, n, CHUNK, h), 3, 1)
    beta = jnp.moveaxis(beta.reshape(b, n, CHUNK, h), 3, 1)
    gc = jnp.cumsum(g, axis=-1)
    incl = jnp.tril(jnp.ones((CHUNK, CHUNK), dtype=bool))
    strict = jnp.tril(jnp.ones((CHUNK, CHUNK), dtype=bool), -1)
    decay = jnp.exp(jnp.where(incl, gc[..., :, None] - gc[..., None, :], -jnp.inf))
    kk = jnp.einsum('bhnid,bhnjd->bhnij', k, k)
    a = jnp.where(strict, beta[..., :, None] * kk * decay, 0.0)
    eye = jnp.eye(CHUNK, dtype=a.dtype)
    rhs = jnp.concatenate([beta[..., None] * v, (beta * jnp.exp(gc))[..., None] * k], axis=-1)
    sol = lax.linalg.triangular_solve(eye + a, rhs, left_side=True, lower=True, unit_diagonal=True)
    u_bar, w = sol[..., :dv], sol[..., dv:]
    qk = jnp.einsum('bhnid,bhnjd->bhnij', q, k) * decay
    q_dec = q * jnp.exp(gc)[..., None]
    k_dec = k * jnp.exp(gc[..., -1:] - gc)[..., None]
    g_last = jnp.exp(gc[..., -1])

    def step(state, xs):
        u_bar_c, w_c, qk_c, q_c, k_c, gl_c = xs
        u = u_bar_c - jnp.einsum('bhck,bhkv->bhcv', w_c, state)
        o = jnp.einsum('bhck,bhkv->bhcv', q_c, state) + jnp.einsum('bhcj,bhjv->bhcv', qk_c, u)
        state = gl_c[..., None, None] * state + jnp.einsum('bhck,bhcv->bhkv', k_c, u)
        return state, o

    xs = tuple(jnp.moveaxis(t, 2, 0) for t in (u_bar, w, qk, q_dec, k_dec, g_last))
    state0 = jnp.zeros((b, h, dk, dv), jnp.float32)
    _, o = lax.scan(step, state0, xs)
    return jnp.transpose(o, (1, 0, 3, 2, 4)).reshape(b, s, h, dv)


def setup_inputs(seed: int = 0) -> dict:
    key = jax.random.key(seed)
    ks = jax.random.split(key, 16)
    f32 = jnp.float32

    def nrm(k, shape, fan_in):
        return jax.random.normal(k, shape, f32) * (fan_in ** -0.5)

    def gain(k, shape):
        return 1.0 + 0.02 * jax.random.normal(k, shape, f32)

    x = jax.random.normal(ks[0], (BATCH, SEQ, D_MODEL), f32)
    a_log = jnp.log(jax.random.uniform(ks[4], (DEPTH, N_HEADS_B), f32, minval=1.0, maxval=16.0))
    dt = jnp.exp(jax.random.uniform(ks[5], (DEPTH, N_HEADS_B), f32, minval=math.log(1e-3), maxval=math.log(1e-1)))
    dt_bias = dt + jnp.log(-jnp.expm1(-dt))
    return {
        'x': x,
        'norm_mix': gain(ks[1], (DEPTH, D_MODEL)),
        'w_in': nrm(ks[2], (DEPTH, D_MODEL, D_IN), D_MODEL),
        'conv_w': nrm(ks[3], (DEPTH, CONV_WIDTH, 2 * D_KEY_B + D_VAL_B), CONV_WIDTH),
        'a_log': a_log,
        'dt_bias': dt_bias,
        'dn_norm': gain(ks[6], (DEPTH, HEAD_DIM_V)),
        'w_proj_attn': nrm(ks[7], (DEPTH, D_ATTN_OUT, D_MODEL), D_ATTN_OUT),
        'w_proj_delta': nrm(ks[8], (DEPTH, D_VAL_B, D_MODEL), D_VAL_B),
        'w_out': nrm(ks[9], (DEPTH, D_MODEL, D_MODEL), D_MODEL),
        'norm_ffn': gain(ks[10], (DEPTH, D_MODEL)),
        'w_gate': nrm(ks[11], (DEPTH, D_MODEL, D_FF), D_MODEL),
        'w_up': nrm(ks[12], (DEPTH, D_MODEL, D_FF), D_MODEL),
        'w_down': nrm(ks[13], (DEPTH, D_FF, D_MODEL), D_FF),
        'norm_final': gain(ks[14], (D_MODEL,)),
    }


def reference(x, norm_mix, w_in, conv_w, a_log, dt_bias, dn_norm, w_proj_attn, w_proj_delta,
              w_out, norm_ffn, w_gate, w_up, w_down, norm_final):
    b, s, _ = x.shape
    f32 = jnp.float32
    slopes = alibi_slopes()
    split_at = np.cumsum(IN_SPLITS)[:-1].tolist()
    h = x
    for layer in range(DEPTH):
        u = rmsnorm(h, norm_mix[layer])
        proj = u @ w_in[layer]
        qa, ka, va, qd, kd, vd, z, beta_raw, a_raw, gate_raw = jnp.split(proj, split_at, axis=-1)

        qa = qa.reshape(b, s, N_HEADS_A, HEAD_DIM_A)
        ka = ka.reshape(b, s, N_HEADS_A, HEAD_DIM_A)
        va = va.reshape(b, s, N_HEADS_A, HEAD_DIM_A)
        outs, lses = [], []
        for gi, (window, dilation) in enumerate(ATTN_PATTERNS):
            hs = slice(gi * HEADS_PER_GROUP, (gi + 1) * HEADS_PER_GROUP)
            o, lse = dilated_window_attention(qa[:, :, hs], ka[:, :, hs], va[:, :, hs], slopes[hs], window, dilation)
            outs.append(o)
            lses.append(lse)
        mix_w = jax.nn.softmax(jnp.stack(lses), axis=0)
        y_a = jnp.einsum('gbsh,gbshd->bshd', mix_w, jnp.stack(outs)).reshape(b, s, D_ATTN_OUT).astype(x.dtype)
        y_a = y_a @ w_proj_attn[layer]

        qkv = jax.nn.silu(causal_depthwise_conv(jnp.concatenate([qd, kd, vd], axis=-1), conv_w[layer]))
        qd, kd, vd = jnp.split(qkv, [D_KEY_B, 2 * D_KEY_B], axis=-1)
        qd = l2norm(qd.reshape(b, s, N_HEADS_B, HEAD_DIM_K).astype(f32)) * (HEAD_DIM_K ** -0.5)
        kd = l2norm(kd.reshape(b, s, N_HEADS_B, HEAD_DIM_K).astype(f32))
        vd = vd.reshape(b, s, N_HEADS_B, HEAD_DIM_V).astype(f32)
        beta = jax.nn.sigmoid(beta_raw.astype(f32))
        g = -jnp.exp(a_log[layer].astype(f32)) * jax.nn.softplus(a_raw.astype(f32) + dt_bias[layer].astype(f32))
        o_d = gated_delta_rule(qd, kd, vd, g, beta)
        o_d = rmsnorm(o_d, dn_norm[layer].astype(f32)) * jax.nn.silu(z.reshape(b, s, N_HEADS_B, HEAD_DIM_V).astype(f32))
        y_b = o_d.reshape(b, s, D_VAL_B).astype(x.dtype) @ w_proj_delta[layer]

        gate_a, gate_b = jnp.split(jax.nn.sigmoid(gate_raw), N_BRANCHES, axis=-1)
        h = h + (gate_a * y_a + gate_b * y_b) @ w_out[layer]

        hn = rmsnorm(h, norm_ffn[layer])
        h = h + (jax.nn.silu(hn @ w_gate[layer]) * (hn @ w_up[layer])) @ w_down[layer]
    return rmsnorm(h, norm_final)
```

```python
import functools

import numpy as np
import jax
import jax.numpy as jnp
from jax import lax
from jax.experimental import pallas as pl
from jax.experimental.pallas import tpu as pltpu

F32 = jnp.float32
BF16 = jnp.bfloat16

EPS = 1e-6
LANES = 128
HEAD_DIM = 128
ATTN_PATTERNS = ((128, 1), (512, 4), (2048, 16))
HEADS_PER_GROUP = 4
N_HEADS_A = len(ATTN_PATTERNS) * HEADS_PER_GROUP
D_ATTN = N_HEADS_A * HEAD_DIM
D_ATTN_OUT = HEADS_PER_GROUP * HEAD_DIM
ATTN_WINDOW_SUB = 128
N_HEADS_B = 8
D_DELTA = N_HEADS_B * HEAD_DIM
CONV_WIDTH = 4

VMEM_LIMIT = 56 * 1024 * 1024

TM_PROJ = 1024
TQ_ATTN = 512
DELTA_CHUNK = 128
DELTA_TILE = 512
DELTA_HEADS_PER_STEP = 4
TM_MERGE = 512
TM_FFN = 256


def _sigmoid(v):
    return 1.0 / (1.0 + jnp.exp(-v))


def _silu(v):
    return v * _sigmoid(v)


def _dot(a, b):
    return jnp.dot(a, b, preferred_element_type=F32)


def _dot_nt(a, b):
    return lax.dot_general(a, b, (((1,), (1,)), ((), ())), preferred_element_type=F32)


def _dot_tn(a, b):
    return lax.dot_general(a, b, (((0,), (0,)), ((), ())), preferred_element_type=F32)


def _norm_matmul_kernel(x_ref, g_ref, w_ref, o_ref, xn_ref):
    @pl.when(pl.program_id(1) == 0)
    def _():
        x = x_ref[...]
        ms = jnp.mean(x * x, axis=-1, keepdims=True)
        xn_ref[...] = (x * lax.rsqrt(ms + EPS) * g_ref[...]).astype(BF16)

    o_ref[...] = _dot(xn_ref[...], w_ref[...]).astype(o_ref.dtype)


def _norm_matmul(x, gain, w, out_dtype, tn):
    s, d = x.shape
    n = w.shape[1]
    return pl.pallas_call(
        _norm_matmul_kernel,
        out_shape=jax.ShapeDtypeStruct((s, n), out_dtype),
        grid=(s // TM_PROJ, n // tn),
        in_specs=[
            pl.BlockSpec((TM_PROJ, d), lambda i, j: (i, 0)),
            pl.BlockSpec((1, d), lambda i, j: (0, 0)),
            pl.BlockSpec((d, tn), lambda i, j: (0, j)),
        ],
        out_specs=pl.BlockSpec((TM_PROJ, tn), lambda i, j: (i, j)),
        scratch_shapes=[pltpu.VMEM((TM_PROJ, d), BF16)],
        compiler_params=pltpu.CompilerParams(
            dimension_semantics=("parallel", "arbitrary"), vmem_limit_bytes=VMEM_LIMIT),
        name="norm_matmul",
    )(x, gain.reshape(1, d), w)


def _attn_bias_table(group):
    _, dilation = ATTN_PATTERNS[group]
    heads = np.arange(group * HEADS_PER_GROUP, (group + 1) * HEADS_PER_GROUP, dtype=np.float32)
    slopes = np.exp2(np.float32(-8.0) * (heads + 1) / np.float32(N_HEADS_A)).astype(np.float32)
    i = np.arange(LANES)[:, None]
    c = np.arange(2 * LANES)[None, :]
    delta = LANES + i - c
    valid = (delta >= 0) & (delta <= ATTN_WINDOW_SUB)
    bias = -(slopes * np.float32(dilation))[:, None, None] * delta.astype(np.float32)[None]
    return jnp.asarray(np.where(valid[None], bias, -np.inf).astype(np.float32))


def _attn_kernel(q_ref, kc_ref, kp_ref, vc_ref, vp_ref, bias_ref, o_ref, lse_ref):
    first_tile = pl.program_id(1) == 0
    scale = HEAD_DIM ** -0.5
    lane = lax.broadcasted_iota(jnp.int32, (LANES, LANES), 1)
    key_col = lax.broadcasted_iota(jnp.int32, (LANES, 2 * LANES), 1)
    for b in range(TQ_ATTN // LANES):
        rows = slice(b * LANES, (b + 1) * LANES)
        lse_tile = jnp.zeros((LANES, LANES), F32)
        for j in range(HEADS_PER_GROUP):
            cols = slice(j * HEAD_DIM, (j + 1) * HEAD_DIM)
            q = q_ref[rows, cols]
            if b == 0:
                k2 = jnp.concatenate([kp_ref[:, cols], kc_ref[0:LANES, cols]], axis=0)
                v2 = jnp.concatenate([vp_ref[:, cols], vc_ref[0:LANES, cols]], axis=0)
            else:
                k2 = kc_ref[(b - 1) * LANES:(b + 1) * LANES, cols]
                v2 = vc_ref[(b - 1) * LANES:(b + 1) * LANES, cols]
            s = _dot_nt(q, k2) * scale + bias_ref[j]
            if b == 0:
                s = jnp.where(jnp.logical_and(first_tile, key_col < LANES), -jnp.inf, s)
            m = jnp.max(s, axis=-1, keepdims=True)
            p = jnp.exp(s - m)
            den = jnp.sum(p, axis=-1, keepdims=True)
            o_ref[rows, cols] = _dot(p.astype(BF16), v2) / den
            lse_tile = jnp.where(lane == j, m + jnp.log(den), lse_tile)
        lse_ref[rows, :] = lse_tile


def _attn_group(qkv, group):
    s = qkv.shape[0]
    _, d = ATTN_PATTERNS[group]
    r = s // d
    a = qkv.reshape(r, d * 3 * D_ATTN)
    blocks_per_sub = 3 * D_ATTN // D_ATTN_OUT
    halo_step = TQ_ATTN // LANES

    def cur(section):
        return pl.BlockSpec((TQ_ATTN, D_ATTN_OUT),
                            lambda sub, n: (n, sub * blocks_per_sub + 3 * section + group))

    def prev(section):
        return pl.BlockSpec((LANES, D_ATTN_OUT),
                            lambda sub, n: (jnp.maximum(n * halo_step - 1, 0),
                                            sub * blocks_per_sub + 3 * section + group))

    o, lse = pl.pallas_call(
        _attn_kernel,
        out_shape=(jax.ShapeDtypeStruct((r, d * D_ATTN_OUT), F32),
                   jax.ShapeDtypeStruct((r, d * LANES), F32)),
        grid=(d, r // TQ_ATTN),
        in_specs=[cur(0), cur(1), prev(1), cur(2), prev(2),
                  pl.BlockSpec((HEADS_PER_GROUP, LANES, 2 * LANES), lambda sub, n: (0, 0, 0))],
        out_specs=(pl.BlockSpec((TQ_ATTN, D_ATTN_OUT), lambda sub, n: (n, sub)),
                   pl.BlockSpec((TQ_ATTN, LANES), lambda sub, n: (n, sub))),
        compiler_params=pltpu.CompilerParams(
            dimension_semantics=("parallel", "arbitrary"), vmem_limit_bytes=VMEM_LIMIT),
        name=f"attn_group{group}",
    )(a, a, a, a, a, _attn_bias_table(group))
    return o.reshape(s, D_ATTN_OUT), lse.reshape(s, LANES)


def _delta_kernel(qx_ref, kx_ref, vx_ref, qh_ref, kh_ref, vh_ref, z_ref, ba_ref,
                  cwq_ref, cwk_ref, cwv_ref, alog_ref, dtb_ref, dn_ref, tri_ref, out_ref,
                  xe_ref, state_ref, kn_ref, kbeta_ref, qn_ref, qd_ref, kd_ref, vb_ref, kbe_ref,
                  gc_ref, gl_ref, gct_ref, o_ref):
    c_len = DELTA_CHUNK
    n_heads = DELTA_HEADS_PER_STEP
    tile = pl.program_id(1)

    @pl.when(tile == 0)
    def _():
        state_ref[...] = jnp.zeros_like(state_ref)

    def conv_silu(x_ref, halo_ref, w_ref):
        xe_ref[0:8, :] = jnp.where(tile == 0, 0.0, halo_ref[...])
        xe_ref[8:, :] = x_ref[...]
        w = w_ref[...]
        y = w[0:1, :] * xe_ref[pl.ds(8 - (CONV_WIDTH - 1), DELTA_TILE), :]
        for j in range(1, CONV_WIDTH):
            y = y + w[j:j + 1, :] * xe_ref[pl.ds(8 - (CONV_WIDTH - 1) + j, DELTA_TILE), :]
        return _silu(y)

    def l2n(v):
        return v * lax.rsqrt(jnp.sum(v * v, axis=-1, keepdims=True) + EPS)

    ba = ba_ref[...]
    lane = lax.broadcasted_iota(jnp.int32, ba.shape, 1)
    pre = ba + dtb_ref[...]
    softplus = jnp.maximum(pre, 0.0) + jnp.log(1.0 + jnp.exp(-jnp.abs(pre)))
    bg = jnp.where(lane < 8, _sigmoid(ba), -jnp.exp(alog_ref[...]) * softplus)
    hi = bg.astype(BF16)
    r1 = bg - hi.astype(F32)
    mid = r1.astype(BF16)
    lo = (r1 - mid.astype(F32)).astype(BF16)
    sums = _dot(tri_ref[...], jnp.concatenate([hi, mid, lo], axis=1))
    sums = sums[:, 0:LANES] + sums[:, LANES:2 * LANES] + sums[:, 2 * LANES:3 * LANES]
    gc = sums[0:DELTA_TILE]
    gl = sums[DELTA_TILE:2 * DELTA_TILE]
    gc_ref[...] = gc
    gl_ref[...] = gl
    for c in range(DELTA_TILE // c_len):
        gct_ref[c] = jnp.transpose(gc[c * c_len:(c + 1) * c_len, :])

    q_all = conv_silu(qx_ref, qh_ref, cwq_ref)
    for h in range(n_heads):
        cols = slice(h * HEAD_DIM, (h + 1) * HEAD_DIM)
        qn = l2n(q_all[:, cols]) * (HEAD_DIM ** -0.5)
        qn_ref[:, cols] = qn.astype(BF16)
        qd_ref[:, cols] = (qn * jnp.exp(gc[:, 8 + h:9 + h])).astype(BF16)
    k_all = conv_silu(kx_ref, kh_ref, cwk_ref)
    for h in range(n_heads):
        cols = slice(h * HEAD_DIM, (h + 1) * HEAD_DIM)
        kn = l2n(k_all[:, cols])
        beta = bg[:, h:h + 1]
        gch = gc[:, 8 + h:9 + h]
        kn_ref[:, cols] = kn.astype(BF16)
        kbeta_ref[:, cols] = (kn * beta).astype(BF16)
        kbe_ref[:, cols] = (kn * (beta * jnp.exp(gch))).astype(BF16)
        kd_ref[:, cols] = (kn * jnp.exp(gl[:, 8 + h:9 + h] - gch)).astype(BF16)
    v_all = conv_silu(vx_ref, vh_ref, cwv_ref)
    for h in range(n_heads):
        cols = slice(h * HEAD_DIM, (h + 1) * HEAD_DIM)
        vb_ref[:, cols] = (v_all[:, cols] * bg[:, h:h + 1]).astype(BF16)

    row = lax.broadcasted_iota(jnp.int32, (c_len, c_len), 0)
    col = lax.broadcasted_iota(jnp.int32, (c_len, c_len), 1)
    lower = row >= col
    strict = row > col
    eye = jnp.where(row == col, 1.0, 0.0).astype(F32)
    n_levels = c_len.bit_length() - 2

    def chunk_body(c, carry):
        r = pl.ds(pl.multiple_of(c * c_len, c_len), c_len)
        for h in range(n_heads):
            cols = slice(h * HEAD_DIM, (h + 1) * HEAD_DIM)
            kq = _dot_nt(jnp.concatenate([kbeta_ref[r, cols], qn_ref[r, cols]], axis=0), kn_ref[r, cols])
            g_col = gc_ref[r, 8 + h:9 + h]
            g_row = gct_ref[c, 8 + h:9 + h, :]
            decay = jnp.exp(jnp.where(lower, g_col - g_row, -jnp.inf))
            a = jnp.where(strict, kq[0:c_len] * decay, 0.0)
            qk = kq[c_len:2 * c_len] * decay
            p = -a
            x = eye + p
            for _ in range(n_levels):
                pb = p.astype(BF16)
                p = _dot(pb, pb)
                x = x + _dot(x.astype(BF16), p.astype(BF16))
            sol = _dot(x.astype(BF16), jnp.concatenate([vb_ref[r, cols], kbe_ref[r, cols]], axis=1))
            u_bar = sol[:, 0:HEAD_DIM]
            w = sol[:, HEAD_DIM:2 * HEAD_DIM]
            state = state_ref[h]
            ws = _dot(jnp.concatenate([w.astype(BF16), qd_ref[r, cols]], axis=0), state.astype(BF16))
            u = (u_bar - ws[0:c_len]).astype(BF16)
            o_ref[r, cols] = ws[c_len:2 * c_len] + _dot(qk.astype(BF16), u)
            g_last = jnp.exp(jnp.broadcast_to(gl_ref[r, 8 + h:9 + h], (c_len, HEAD_DIM))[0:1, :])
            state_ref[h] = jnp.broadcast_to(g_last, (HEAD_DIM, HEAD_DIM)) * state + _dot_tn(kd_ref[r, cols], u)
        return carry

    lax.fori_loop(0, DELTA_TILE // c_len, chunk_body, 0)

    dn = dn_ref[...]
    for h in range(n_heads):
        cols = slice(h * HEAD_DIM, (h + 1) * HEAD_DIM)
        o = o_ref[:, cols]
        o = o * lax.rsqrt(jnp.mean(o * o, axis=-1, keepdims=True) + EPS) * dn
        out_ref[:, cols] = o * _silu(z_ref[:, cols])


def _delta_tri_matrix():
    i = np.arange(DELTA_TILE)[:, None]
    t = np.arange(DELTA_TILE)[None, :]
    same = (i // DELTA_CHUNK) == (t // DELTA_CHUNK)
    return jnp.asarray(np.concatenate([same & (t <= i), same], axis=0).astype(np.float32), dtype=BF16)


def _delta_rule(rest, ba, conv_w, a_log, dt_bias, dn_norm):
    s = rest.shape[0]
    hps = DELTA_HEADS_PER_STEP
    width = hps * HEAD_DIM
    groups = N_HEADS_B // hps
    per_section = D_DELTA // width
    halo_blocks = DELTA_TILE // 8

    def main(section):
        return pl.BlockSpec((DELTA_TILE, width), lambda g, t: (t, section * per_section + g))

    def halo(section):
        return pl.BlockSpec((8, width), lambda g, t: (jnp.maximum(t * halo_blocks - 1, 0),
                                                      section * per_section + g))

    def cw(section):
        return pl.BlockSpec((CONV_WIDTH, width), lambda g, t: (0, section * per_section + g))

    def lane_table(v):
        tab = jnp.zeros((groups, LANES), F32).at[:, 8:8 + hps].set(v.reshape(groups, hps).astype(F32))
        return tab.reshape(1, groups * LANES)

    bf = lambda: pltpu.VMEM((DELTA_TILE, width), BF16)
    return pl.pallas_call(
        _delta_kernel,
        out_shape=jax.ShapeDtypeStruct((s, D_DELTA), F32),
        grid=(groups, s // DELTA_TILE),
        in_specs=[main(0), main(1), main(2), halo(0), halo(1), halo(2), main(3),
                  pl.BlockSpec((DELTA_TILE, LANES), lambda g, t: (t, g)),
                  cw(0), cw(1), cw(2),
                  pl.BlockSpec((1, LANES), lambda g, t: (0, g)),
                  pl.BlockSpec((1, LANES), lambda g, t: (0, g)),
                  pl.BlockSpec((1, HEAD_DIM), lambda g, t: (0, 0)),
                  pl.BlockSpec((2 * DELTA_TILE, DELTA_TILE), lambda g, t: (0, 0))],
        out_specs=pl.BlockSpec((DELTA_TILE, width), lambda g, t: (t, g)),
        scratch_shapes=[
            pltpu.VMEM((DELTA_TILE + 8, width), F32),
            pltpu.VMEM((hps, HEAD_DIM, HEAD_DIM), F32),
            bf(), bf(), bf(), bf(), bf(), bf(), bf(),
            pltpu.VMEM((DELTA_TILE, LANES), F32),
            pltpu.VMEM((DELTA_TILE, LANES), F32),
            pltpu.VMEM((DELTA_TILE // DELTA_CHUNK, LANES, DELTA_CHUNK), F32),
            pltpu.VMEM((DELTA_TILE, width), F32),
        ],
        compiler_params=pltpu.CompilerParams(
            dimension_semantics=("parallel", "arbitrary"), vmem_limit_bytes=VMEM_LIMIT),
        name="delta_rule",
    )(rest, rest, rest, rest, rest, rest, rest, ba, conv_w, conv_w, conv_w,
      lane_table(a_log), lane_table(dt_bias), dn_norm.reshape(1, HEAD_DIM).astype(F32),
      _delta_tri_matrix())


def _merge_kernel(o0_ref, o1_ref, o2_ref, l0_ref, l1_ref, l2_ref, od_ref, ga_ref, gb_ref, x_ref,
                  wpa_ref, wpd_ref, wo_ref, out_ref):
    l0, l1, l2 = l0_ref[...], l1_ref[...], l2_ref[...]
    mx = jnp.maximum(jnp.maximum(l0, l1), l2)
    e0, e1, e2 = jnp.exp(l0 - mx), jnp.exp(l1 - mx), jnp.exp(l2 - mx)
    inv = 1.0 / (e0 + e1 + e2)
    w0, w1, w2 = e0 * inv, e1 * inv, e2 * inv
    parts = []
    for j in range(HEADS_PER_GROUP):
        cols = slice(j * HEAD_DIM, (j + 1) * HEAD_DIM)
        parts.append(w0[:, j:j + 1] * o0_ref[:, cols] + w1[:, j:j + 1] * o1_ref[:, cols]
                     + w2[:, j:j + 1] * o2_ref[:, cols])
    y_a = _dot(jnp.concatenate(parts, axis=1).astype(BF16), wpa_ref[...])
    y_b = _dot(od_ref[...].astype(BF16), wpd_ref[...])
    merged = _sigmoid(ga_ref[...]) * y_a + _sigmoid(gb_ref[...]) * y_b
    out_ref[...] = x_ref[...] + _dot(merged.astype(BF16), wo_ref[...])


def _merge_proj(outs, lses, o_delta, rest, x, w_pa, w_pd, w_o):
    s, d = x.shape
    tm = TM_MERGE
    gate_block = (3 * D_DELTA + D_DELTA) // d
    row = lambda width: pl.BlockSpec((tm, width), lambda i: (i, 0))
    full = lambda a: pl.BlockSpec(a.shape, lambda i: (0, 0))
    return pl.pallas_call(
        _merge_kernel,
        out_shape=jax.ShapeDtypeStruct((s, d), F32),
        grid=(s // tm,),
        in_specs=[row(D_ATTN_OUT)] * 3 + [row(LANES)] * 3 + [
            row(D_DELTA),
            pl.BlockSpec((tm, d), lambda i: (i, gate_block)),
            pl.BlockSpec((tm, d), lambda i: (i, gate_block + 1)),
            row(d), full(w_pa), full(w_pd), full(w_o)],
        out_specs=row(d),
        compiler_params=pltpu.CompilerParams(
            dimension_semantics=("parallel",), vmem_limit_bytes=VMEM_LIMIT),
        name="merge_proj",
    )(*outs, *lses, o_delta, rest, rest, x, w_pa, w_pd, w_o)


def _ffn_kernel(h_ref, g_ref, wg_ref, wu_ref, wd_ref, gf_ref, o_ref, *, final_norm):
    h = h_ref[...]
    hn = (h * lax.rsqrt(jnp.mean(h * h, axis=-1, keepdims=True) + EPS) * g_ref[...]).astype(BF16)
    act = (_silu(_dot(hn, wg_ref[...])) * _dot(hn, wu_ref[...])).astype(BF16)
    y = h + _dot(act, wd_ref[...])
    if final_norm:
        y = y * lax.rsqrt(jnp.mean(y * y, axis=-1, keepdims=True) + EPS) * gf_ref[...]
    o_ref[...] = y


def _ffn(h, gain, w_g, w_u, w_d, gain_final, final_norm):
    s, d = h.shape
    tm = TM_FFN
    full = lambda a: pl.BlockSpec(a.shape, lambda i: (0, 0))
    vec = pl.BlockSpec((1, d), lambda i: (0, 0))
    return pl.pallas_call(
        functools.partial(_ffn_kernel, final_norm=final_norm),
        out_shape=jax.ShapeDtypeStruct((s, d), F32),
        grid=(s // tm,),
        in_specs=[pl.BlockSpec((tm, d), lambda i: (i, 0)), vec, full(w_g), full(w_u), full(w_d), vec],
        out_specs=pl.BlockSpec((tm, d), lambda i: (i, 0)),
        compiler_params=pltpu.CompilerParams(
            dimension_semantics=("parallel",), vmem_limit_bytes=VMEM_LIMIT),
        name="ffn",
    )(h, gain.reshape(1, d), w_g, w_u, w_d, gain_final.reshape(1, d))


def _gate_weight_columns(w_in, col_beta, col_a):
    hps = DELTA_HEADS_PER_STEP
    groups = N_HEADS_B // hps
    d = w_in.shape[0]
    w = jnp.zeros((d, groups, LANES), w_in.dtype)
    w = w.at[:, :, 0:hps].set(w_in[:, col_beta:col_beta + N_HEADS_B].reshape(d, groups, hps))
    w = w.at[:, :, 8:8 + hps].set(w_in[:, col_a:col_a + N_HEADS_B].reshape(d, groups, hps))
    return w.reshape(d, groups * LANES)


def kernel(x, norm_mix, w_in, conv_w, a_log, dt_bias, dn_norm, w_proj_attn, w_proj_delta,
           w_out, norm_ffn, w_gate, w_up, w_down, norm_final):
    b, s, d = x.shape
    assert b == 1 and s % (TM_PROJ * 16) == 0 and d == 8 * LANES
    depth = w_in.shape[0]
    c_attn = 3 * D_ATTN
    c_z = c_attn + 4 * D_DELTA
    c_beta, c_a = c_z, c_z + N_HEADS_B
    c_gate = c_z + 2 * N_HEADS_B
    h = x.reshape(s, d)
    for layer in range(depth):
        wl = w_in[layer]
        w_attn = wl[:, :c_attn].astype(BF16)
        w_rest = jnp.concatenate([wl[:, c_attn:c_z], wl[:, c_gate:]], axis=1).astype(BF16)
        w_ba = _gate_weight_columns(wl, c_beta, c_a).astype(BF16)

        qkv_a = _norm_matmul(h, norm_mix[layer], w_attn, BF16, 1536)
        rest = _norm_matmul(h, norm_mix[layer], w_rest, F32, 1536)
        ba = _norm_matmul(h, norm_mix[layer], w_ba, F32, w_ba.shape[1])

        outs, lses = zip(*[_attn_group(qkv_a, g) for g in range(len(ATTN_PATTERNS))])
        o_delta = _delta_rule(rest, ba, conv_w[layer], a_log[layer], dt_bias[layer], dn_norm[layer])
        h = _merge_proj(outs, lses, o_delta, rest, h, w_proj_attn[layer].astype(BF16),
                        w_proj_delta[layer].astype(BF16), w_out[layer].astype(BF16))
        h = _ffn(h, norm_ffn[layer], w_gate[layer].astype(BF16), w_up[layer].astype(BF16),
                 w_down[layer].astype(BF16), norm_final, layer == depth - 1)
    return h.reshape(b, s, d)
```

```python
import functools

import numpy as np
import jax
import jax.numpy as jnp
from jax import lax
from jax.experimental import pallas as pl
from jax.experimental.pallas import tpu as pltpu

F32 = jnp.float32
BF16 = jnp.bfloat16

EPS = 1e-6
LANES = 128
HEAD_DIM = 128
ATTN_PATTERNS = ((128, 1), (512, 4), (2048, 16))
HEADS_PER_GROUP = 4
N_HEADS_A = len(ATTN_PATTERNS) * HEADS_PER_GROUP
D_ATTN = N_HEADS_A * HEAD_DIM
D_ATTN_OUT = HEADS_PER_GROUP * HEAD_DIM
ATTN_WINDOW_SUB = 128
N_HEADS_B = 8
D_DELTA = N_HEADS_B * HEAD_DIM
CONV_WIDTH = 4

VMEM_LIMIT = 56 * 1024 * 1024

TM_PROJ = 1024
TQ_ATTN = 512
DELTA_CHUNK = 128
DELTA_TILE = 512
DELTA_HEADS_PER_STEP = 4
TM_MERGE = 512
TM_FFN = 256


def _sigmoid(v):
    return 1.0 / (1.0 + jnp.exp(-v))


def _silu(v):
    return v * _sigmoid(v)


def _dot(a, b):
    return jnp.dot(a, b, preferred_element_type=F32)


def _dot_nt(a, b):
    return lax.dot_general(a, b, (((1,), (1,)), ((), ())), preferred_element_type=F32)


def _dot_tn(a, b):
    return lax.dot_general(a, b, (((0,), (0,)), ((), ())), preferred_element_type=F32)


def _norm_matmul_kernel(x_ref, g_ref, w_ref, o_ref, xn_ref):
    @pl.when(pl.program_id(1) == 0)
    def _():
        x = x_ref[...]
        ms = jnp.mean(x * x, axis=-1, keepdims=True)
        xn_ref[...] = (x * lax.rsqrt(ms + EPS) * g_ref[...]).astype(BF16)

    o_ref[...] = _dot(xn_ref[...], w_ref[...]).astype(o_ref.dtype)


def _norm_matmul(x, gain, w, out_dtype, tn):
    s, d = x.shape
    n = w.shape[1]
    return pl.pallas_call(
        _norm_matmul_kernel,
        out_shape=jax.ShapeDtypeStruct((s, n), out_dtype),
        grid=(s // TM_PROJ, n // tn),
        in_specs=[
            pl.BlockSpec((TM_PROJ, d), lambda i, j: (i, 0)),
            pl.BlockSpec((1, d), lambda i, j: (0, 0)),
            pl.BlockSpec((d, tn), lambda i, j: (0, j)),
        ],
        out_specs=pl.BlockSpec((TM_PROJ, tn), lambda i, j: (i, j)),
        scratch_shapes=[pltpu.VMEM((TM_PROJ, d), BF16)],
        compiler_params=pltpu.CompilerParams(
            dimension_semantics=("parallel", "arbitrary"), vmem_limit_bytes=VMEM_LIMIT),
        name="norm_matmul",
    )(x, gain.reshape(1, d), w)


def _attn_bias_table(group):
    _, dilation = ATTN_PATTERNS[group]
    heads = np.arange(group * HEADS_PER_GROUP, (group + 1) * HEADS_PER_GROUP, dtype=np.float32)
    slopes = np.exp2(np.float32(-8.0) * (heads + 1) / np.float32(N_HEADS_A)).astype(np.float32)
    i = np.arange(LANES)[:, None]
    c = np.arange(2 * LANES)[None, :]
    delta = LANES + i - c
    valid = (delta >= 0) & (delta <= ATTN_WINDOW_SUB)
    bias = -(slopes * np.float32(dilation))[:, None, None] * delta.astype(np.float32)[None]
    return jnp.asarray(np.where(valid[None], bias, -np.inf).astype(np.float32))


def _attn_kernel(q_ref, kc_ref, kp_ref, vc_ref, vp_ref, bias_ref, o_ref, lse_ref):
    first_tile = pl.program_id(1) == 0
    scale = HEAD_DIM ** -0.5
    lane = lax.broadcasted_iota(jnp.int32, (LANES, LANES), 1)
    key_col = lax.broadcasted_iota(jnp.int32, (LANES, 2 * LANES), 1)
    for b in range(TQ_ATTN // LANES):
        rows = slice(b * LANES, (b + 1) * LANES)
        lse_tile = jnp.zeros((LANES, LANES), F32)
        for j in range(HEADS_PER_GROUP):
            cols = slice(j * HEAD_DIM, (j + 1) * HEAD_DIM)
            q = q_ref[rows, cols]
            if b == 0:
                k2 = jnp.concatenate([kp_ref[:, cols], kc_ref[0:LANES, cols]], axis=0)
                v2 = jnp.concatenate([vp_ref[:, cols], vc_ref[0:LANES, cols]], axis=0)
            else:
                k2 = kc_ref[(b - 1) * LANES:(b + 1) * LANES, cols]
                v2 = vc_ref[(b - 1) * LANES:(b + 1) * LANES, cols]
            s = _dot_nt(q, k2) * scale + bias_ref[j]
            if b == 0:
                s = jnp.where(jnp.logical_and(first_tile, key_col < LANES), -jnp.inf, s)
            m = jnp.max(s, axis=-1, keepdims=True)
            p = jnp.exp(s - m)
            den = jnp.sum(p, axis=-1, keepdims=True)
            o_ref[rows, cols] = _dot(p.astype(BF16), v2) / den
            lse_tile = jnp.where(lane == j, m + jnp.log(den), lse_tile)
        lse_ref[rows, :] = lse_tile


def _attn_group(qkv, group):
    s = qkv.shape[0]
    _, d = ATTN_PATTERNS[group]
    r = s // d
    a = qkv.reshape(r, d * 3 * D_ATTN)
    blocks_per_sub = 3 * D_ATTN // D_ATTN_OUT
    halo_step = TQ_ATTN // LANES

    def cur(section):
        return pl.BlockSpec((TQ_ATTN, D_ATTN_OUT),
                            lambda sub, n: (n, sub * blocks_per_sub + 3 * section + group))

    def prev(section):
        return pl.BlockSpec((LANES, D_ATTN_OUT),
                            lambda sub, n: (jnp.maximum(n * halo_step - 1, 0),
                                            sub * blocks_per_sub + 3 * section + group))

    o, lse = pl.pallas_call(
        _attn_kernel,
        out_shape=(jax.ShapeDtypeStruct((r, d * D_ATTN_OUT), F32),
                   jax.ShapeDtypeStruct((r, d * LANES), F32)),
        grid=(d, r // TQ_ATTN),
        in_specs=[cur(0), cur(1), prev(1), cur(2), prev(2),
                  pl.BlockSpec((HEADS_PER_GROUP, LANES, 2 * LANES), lambda sub, n: (0, 0, 0))],
        out_specs=(pl.BlockSpec((TQ_ATTN, D_ATTN_OUT), lambda sub, n: (n, sub)),
                   pl.BlockSpec((TQ_ATTN, LANES), lambda sub, n: (n, sub))),
        compiler_params=pltpu.CompilerParams(
            dimension_semantics=("parallel", "arbitrary"), vmem_limit_bytes=VMEM_LIMIT),
        name=f"attn_group{group}",
    )(a, a, a, a, a, _attn_bias_table(group))
    return o.reshape(s, D_ATTN_OUT), lse.reshape(s, LANES)


def _delta_kernel(qx_ref, kx_ref, vx_ref, qh_ref, kh_ref, vh_ref, z_ref, ba_ref,
                  cwq_ref, cwk_ref, cwv_ref, alog_ref, dtb_ref, dn_ref, tri_ref, out_ref,
                  xe_ref, state_ref, kn_ref, kbeta_ref, qn_ref, qd_ref, kd_ref, vb_ref, kbe_ref,
                  gc_ref, gl_ref, gct_ref, o_ref, b_ref, inv_ref, m_ref, qk_ref, ubar_ref, w_ref):
    c_len = DELTA_CHUNK
    n_heads = DELTA_HEADS_PER_STEP
    tile = pl.program_id(1)

    @pl.when(tile == 0)
    def _():
        state_ref[...] = jnp.zeros_like(state_ref)

    def conv_silu(x_ref, halo_ref, w_ref):
        xe_ref[0:8, :] = jnp.where(tile == 0, 0.0, halo_ref[...])
        xe_ref[8:, :] = x_ref[...]
        w = w_ref[...]
        y = w[0:1, :] * xe_ref[pl.ds(8 - (CONV_WIDTH - 1), DELTA_TILE), :]
        for j in range(1, CONV_WIDTH):
            y = y + w[j:j + 1, :] * xe_ref[pl.ds(8 - (CONV_WIDTH - 1) + j, DELTA_TILE), :]
        return _silu(y)

    def l2n(v):
        return v * lax.rsqrt(jnp.sum(v * v, axis=-1, keepdims=True) + EPS)

    ba = ba_ref[...]
    lane = lax.broadcasted_iota(jnp.int32, ba.shape, 1)
    pre = ba + dtb_ref[...]
    softplus = jnp.maximum(pre, 0.0) + jnp.log(1.0 + jnp.exp(-jnp.abs(pre)))
    bg = jnp.where(lane < 8, _sigmoid(ba), -jnp.exp(alog_ref[...]) * softplus)
    hi = bg.astype(BF16)
    r1 = bg - hi.astype(F32)
    mid = r1.astype(BF16)
    lo = (r1 - mid.astype(F32)).astype(BF16)
    sums = _dot(tri_ref[...], jnp.concatenate([hi, mid, lo], axis=1))
    sums = sums[:, 0:LANES] + sums[:, LANES:2 * LANES] + sums[:, 2 * LANES:3 * LANES]
    gc = sums[0:DELTA_TILE]
    gl = sums[DELTA_TILE:2 * DELTA_TILE]
    gc_ref[...] = gc
    gl_ref[...] = gl
    for c in range(DELTA_TILE // c_len):
        gct_ref[c] = jnp.transpose(gc[c * c_len:(c + 1) * c_len, :])

    q_all = conv_silu(qx_ref, qh_ref, cwq_ref)
    for h in range(n_heads):
        cols = slice(h * HEAD_DIM, (h + 1) * HEAD_DIM)
        qn = l2n(q_all[:, cols]) * (HEAD_DIM ** -0.5)
        qn_ref[:, cols] = qn.astype(BF16)
        qd_ref[:, cols] = (qn * jnp.exp(gc[:, 8 + h:9 + h])).astype(BF16)
    k_all = conv_silu(kx_ref, kh_ref, cwk_ref)
    for h in range(n_heads):
        cols = slice(h * HEAD_DIM, (h + 1) * HEAD_DIM)
        kn = l2n(k_all[:, cols])
        beta = bg[:, h:h + 1]
        gch = gc[:, 8 + h:9 + h]
        kn_ref[:, cols] = kn.astype(BF16)
        kbeta_ref[:, cols] = (kn * beta).astype(BF16)
        kbe_ref[:, cols] = (kn * (beta * jnp.exp(gch))).astype(BF16)
        kd_ref[:, cols] = (kn * jnp.exp(gl[:, 8 + h:9 + h] - gch)).astype(BF16)
    v_all = conv_silu(vx_ref, vh_ref, cwv_ref)
    for h in range(n_heads):
        cols = slice(h * HEAD_DIM, (h + 1) * HEAD_DIM)
        vb_ref[:, cols] = (v_all[:, cols] * bg[:, h:h + 1]).astype(BF16)

    row = lax.broadcasted_iota(jnp.int32, (c_len, c_len), 0)
    col = lax.broadcasted_iota(jnp.int32, (c_len, c_len), 1)
    lower = row >= col
    strict = row > col
    eye = jnp.where(row == col, 1.0, 0.0).astype(F32)
    pair = jnp.right_shift(row, 1) == jnp.right_shift(col, 1)
    n_levels = c_len.bit_length() - 2

    n_chunks = DELTA_TILE // c_len
    items = [(c, h) for c in range(n_chunks) for h in range(n_heads)]

    def rows_of(c):
        return slice(c * c_len, (c + 1) * c_len)

    def cols_of(h):
        return slice(h * HEAD_DIM, (h + 1) * HEAD_DIM)

    for i, (c, h) in enumerate(items):
        r, cols = rows_of(c), cols_of(h)
        kq = _dot_nt(jnp.concatenate([kbeta_ref[r, cols], qn_ref[r, cols]], axis=0), kn_ref[r, cols])
        g_col = gc_ref[r, 8 + h:9 + h]
        g_row = gct_ref[c, 8 + h:9 + h, :]
        decay = jnp.exp(jnp.where(lower, g_col - g_row, -jnp.inf))
        p = jnp.where(strict, -(kq[0:c_len] * decay), 0.0)
        b_ref[i] = p.astype(BF16)
        inv_ref[i] = (eye + jnp.where(pair, p, 0.0)).astype(BF16)
        qk_ref[i] = (kq[c_len:2 * c_len] * decay).astype(BF16)

    def level(l, carry):
        lb = l + 1
        off = jnp.logical_and(jnp.right_shift(row, lb) == jnp.right_shift(col, lb) + 1,
                              jnp.bitwise_and(jnp.right_shift(col, lb), 1) == 0)
        for i in range(len(items)):
            m_ref[i] = jnp.where(off, _dot(b_ref[i], inv_ref[i]), 0.0).astype(BF16)
        for i in range(len(items)):
            d = inv_ref[i]
            inv_ref[i] = d + _dot(d, m_ref[i]).astype(BF16)
        return carry

    lax.fori_loop(0, n_levels, level, 0)

    for i, (c, h) in enumerate(items):
        r, cols = rows_of(c), cols_of(h)
        sol = _dot(inv_ref[i], jnp.concatenate([vb_ref[r, cols], kbe_ref[r, cols]], axis=1))
        ubar_ref[r, cols] = sol[:, 0:HEAD_DIM]
        w_ref[r, cols] = sol[:, HEAD_DIM:2 * HEAD_DIM].astype(BF16)

    for c in range(n_chunks):
        r = rows_of(c)
        ws = [_dot(jnp.concatenate([w_ref[r, cols_of(h)], qd_ref[r, cols_of(h)]], axis=0),
                   state_ref[h].astype(BF16)) for h in range(n_heads)]
        us = [(ubar_ref[r, cols_of(h)] - ws[h][0:c_len]).astype(BF16) for h in range(n_heads)]
        for h in range(n_heads):
            cols = cols_of(h)
            o_ref[r, cols] = ws[h][c_len:2 * c_len] + _dot(qk_ref[c * n_heads + h], us[h])
            g_last = jnp.exp(jnp.broadcast_to(gl_ref[r, 8 + h:9 + h], (c_len, HEAD_DIM))[0:1, :])
            state_ref[h] = (jnp.broadcast_to(g_last, (HEAD_DIM, HEAD_DIM)) * state_ref[h]
                            + _dot_tn(kd_ref[r, cols], us[h]))

    dn = dn_ref[...]
    for h in range(n_heads):
        cols = slice(h * HEAD_DIM, (h + 1) * HEAD_DIM)
        o = o_ref[:, cols]
        o = o * lax.rsqrt(jnp.mean(o * o, axis=-1, keepdims=True) + EPS) * dn
        out_ref[:, cols] = o * _silu(z_ref[:, cols])


def _delta_tri_matrix():
    i = np.arange(DELTA_TILE)[:, None]
    t = np.arange(DELTA_TILE)[None, :]
    same = (i // DELTA_CHUNK) == (t // DELTA_CHUNK)
    return jnp.asarray(np.concatenate([same & (t <= i), same], axis=0).astype(np.float32), dtype=BF16)


def _delta_rule(rest, ba, conv_w, a_log, dt_bias, dn_norm):
    s = rest.shape[0]
    hps = DELTA_HEADS_PER_STEP
    width = hps * HEAD_DIM
    groups = N_HEADS_B // hps
    per_section = D_DELTA // width
    halo_blocks = DELTA_TILE // 8
    n_items = (DELTA_TILE // DELTA_CHUNK) * hps

    def main(section):
        return pl.BlockSpec((DELTA_TILE, width), lambda g, t: (t, section * per_section + g))

    def halo(section):
        return pl.BlockSpec((8, width), lambda g, t: (jnp.maximum(t * halo_blocks - 1, 0),
                                                      section * per_section + g))

    def cw(section):
        return pl.BlockSpec((CONV_WIDTH, width), lambda g, t: (0, section * per_section + g))

    def lane_table(v):
        tab = jnp.zeros((groups, LANES), F32).at[:, 8:8 + hps].set(v.reshape(groups, hps).astype(F32))
        return tab.reshape(1, groups * LANES)

    bf = lambda: pltpu.VMEM((DELTA_TILE, width), BF16)
    return pl.pallas_call(
        _delta_kernel,
        out_shape=jax.ShapeDtypeStruct((s, D_DELTA), F32),
        grid=(groups, s // DELTA_TILE),
        in_specs=[main(0), main(1), main(2), halo(0), halo(1), halo(2), main(3),
                  pl.BlockSpec((DELTA_TILE, LANES), lambda g, t: (t, g)),
                  cw(0), cw(1), cw(2),
                  pl.BlockSpec((1, LANES), lambda g, t: (0, g)),
                  pl.BlockSpec((1, LANES), lambda g, t: (0, g)),
                  pl.BlockSpec((1, HEAD_DIM), lambda g, t: (0, 0)),
                  pl.BlockSpec((2 * DELTA_TILE, DELTA_TILE), lambda g, t: (0, 0))],
        out_specs=pl.BlockSpec((DELTA_TILE, width), lambda g, t: (t, g)),
        scratch_shapes=[
            pltpu.VMEM((DELTA_TILE + 8, width), F32),
            pltpu.VMEM((hps, HEAD_DIM, HEAD_DIM), F32),
            bf(), bf(), bf(), bf(), bf(), bf(), bf(),
            pltpu.VMEM((DELTA_TILE, LANES), F32),
            pltpu.VMEM((DELTA_TILE, LANES), F32),
            pltpu.VMEM((DELTA_TILE // DELTA_CHUNK, LANES, DELTA_CHUNK), F32),
            pltpu.VMEM((DELTA_TILE, width), F32),
            pltpu.VMEM((n_items, DELTA_CHUNK, DELTA_CHUNK), BF16),
            pltpu.VMEM((n_items, DELTA_CHUNK, DELTA_CHUNK), BF16),
            pltpu.VMEM((n_items, DELTA_CHUNK, DELTA_CHUNK), BF16),
            pltpu.VMEM((n_items, DELTA_CHUNK, DELTA_CHUNK), BF16),
            pltpu.VMEM((DELTA_TILE, width), F32),
            pltpu.VMEM((DELTA_TILE, width), BF16),
        ],
        compiler_params=pltpu.CompilerParams(
            dimension_semantics=("parallel", "arbitrary"), vmem_limit_bytes=VMEM_LIMIT),
        name="delta_rule",
    )(rest, rest, rest, rest, rest, rest, rest, ba, conv_w, conv_w, conv_w,
      lane_table(a_log), lane_table(dt_bias), dn_norm.reshape(1, HEAD_DIM).astype(F32),
      _delta_tri_matrix())


def _merge_kernel(o0_ref, o1_ref, o2_ref, l0_ref, l1_ref, l2_ref, od_ref, ga_ref, gb_ref, x_ref,
                  wpa_ref, wpd_ref, wo_ref, out_ref):
    l0, l1, l2 = l0_ref[...], l1_ref[...], l2_ref[...]
    mx = jnp.maximum(jnp.maximum(l0, l1), l2)
    e0, e1, e2 = jnp.exp(l0 - mx), jnp.exp(l1 - mx), jnp.exp(l2 - mx)
    inv = 1.0 / (e0 + e1 + e2)
    w0, w1, w2 = e0 * inv, e1 * inv, e2 * inv
    parts = []
    for j in range(HEADS_PER_GROUP):
        cols = slice(j * HEAD_DIM, (j + 1) * HEAD_DIM)
        parts.append(w0[:, j:j + 1] * o0_ref[:, cols] + w1[:, j:j + 1] * o1_ref[:, cols]
                     + w2[:, j:j + 1] * o2_ref[:, cols])
    y_a = _dot(jnp.concatenate(parts, axis=1).astype(BF16), wpa_ref[...])
    y_b = _dot(od_ref[...].astype(BF16), wpd_ref[...])
    merged = _sigmoid(ga_ref[...]) * y_a + _sigmoid(gb_ref[...]) * y_b
    out_ref[...] = x_ref[...] + _dot(merged.astype(BF16), wo_ref[...])


def _merge_proj(outs, lses, o_delta, rest, x, w_pa, w_pd, w_o):
    s, d = x.shape
    tm = TM_MERGE
    gate_block = (3 * D_DELTA + D_DELTA) // d
    row = lambda width: pl.BlockSpec((tm, width), lambda i: (i, 0))
    full = lambda a: pl.BlockSpec(a.shape, lambda i: (0, 0))
    return pl.pallas_call(
        _merge_kernel,
        out_shape=jax.ShapeDtypeStruct((s, d), F32),
        grid=(s // tm,),
        in_specs=[row(D_ATTN_OUT)] * 3 + [row(LANES)] * 3 + [
            row(D_DELTA),
            pl.BlockSpec((tm, d), lambda i: (i, gate_block)),
            pl.BlockSpec((tm, d), lambda i: (i, gate_block + 1)),
            row(d), full(w_pa), full(w_pd), full(w_o)],
        out_specs=row(d),
        compiler_params=pltpu.CompilerParams(
            dimension_semantics=("parallel",), vmem_limit_bytes=VMEM_LIMIT),
        name="merge_proj",
    )(*outs, *lses, o_delta, rest, rest, x, w_pa, w_pd, w_o)


def _ffn_kernel(h_ref, g_ref, wg_ref, wu_ref, wd_ref, gf_ref, o_ref, *, final_norm):
    h = h_ref[...]
    hn = (h * lax.rsqrt(jnp.mean(h * h, axis=-1, keepdims=True) + EPS) * g_ref[...]).astype(BF16)
    act = (_silu(_dot(hn, wg_ref[...])) * _dot(hn, wu_ref[...])).astype(BF16)
    y = h + _dot(act, wd_ref[...])
    if final_norm:
        y = y * lax.rsqrt(jnp.mean(y * y, axis=-1, keepdims=True) + EPS) * gf_ref[...]
    o_ref[...] = y


def _ffn(h, gain, w_g, w_u, w_d, gain_final, final_norm):
    s, d = h.shape
    tm = TM_FFN
    full = lambda a: pl.BlockSpec(a.shape, lambda i: (0, 0))
    vec = pl.BlockSpec((1, d), lambda i: (0, 0))
    return pl.pallas_call(
        functools.partial(_ffn_kernel, final_norm=final_norm),
        out_shape=jax.ShapeDtypeStruct((s, d), F32),
        grid=(s // tm,),
        in_specs=[pl.BlockSpec((tm, d), lambda i: (i, 0)), vec, full(w_g), full(w_u), full(w_d), vec],
        out_specs=pl.BlockSpec((tm, d), lambda i: (i, 0)),
        compiler_params=pltpu.CompilerParams(
            dimension_semantics=("parallel",), vmem_limit_bytes=VMEM_LIMIT),
        name="ffn",
    )(h, gain.reshape(1, d), w_g, w_u, w_d, gain_final.reshape(1, d))


def _gate_weight_columns(w_in, col_beta, col_a):
    hps = DELTA_HEADS_PER_STEP
    groups = N_HEADS_B // hps
    d = w_in.shape[0]
    w = jnp.zeros((d, groups, LANES), w_in.dtype)
    w = w.at[:, :, 0:hps].set(w_in[:, col_beta:col_beta + N_HEADS_B].reshape(d, groups, hps))
    w = w.at[:, :, 8:8 + hps].set(w_in[:, col_a:col_a + N_HEADS_B].reshape(d, groups, hps))
    return w.reshape(d, groups * LANES)


def kernel(x, norm_mix, w_in, conv_w, a_log, dt_bias, dn_norm, w_proj_attn, w_proj_delta,
           w_out, norm_ffn, w_gate, w_up, w_down, norm_final):
    b, s, d = x.shape
    assert b == 1 and s % (TM_PROJ * 16) == 0 and d == 8 * LANES
    depth = w_in.shape[0]
    c_attn = 3 * D_ATTN
    c_z = c_attn + 4 * D_DELTA
    c_beta, c_a = c_z, c_z + N_HEADS_B
    c_gate = c_z + 2 * N_HEADS_B
    h = x.reshape(s, d)
    for layer in range(depth):
        wl = w_in[layer]
        w_attn = wl[:, :c_attn].astype(BF16)
        w_rest = jnp.concatenate([wl[:, c_attn:c_z], wl[:, c_gate:]], axis=1).astype(BF16)
        w_ba = _gate_weight_columns(wl, c_beta, c_a).astype(BF16)

        qkv_a = _norm_matmul(h, norm_mix[layer], w_attn, BF16, 1536)
        rest = _norm_matmul(h, norm_mix[layer], w_rest, F32, 1536)
        ba = _norm_matmul(h, norm_mix[layer], w_ba, F32, w_ba.shape[1])

        outs, lses = zip(*[_attn_group(qkv_a, g) for g in range(len(ATTN_PATTERNS))])
        o_delta = _delta_rule(rest, ba, conv_w[layer], a_log[layer], dt_bias[layer], dn_norm[layer])
        h = _merge_proj(outs, lses, o_delta, rest, h, w_proj_attn[layer].astype(BF16),
                        w_proj_delta[layer].astype(BF16), w_out[layer].astype(BF16))
        h = _ffn(h, norm_ffn[layer], w_gate[layer].astype(BF16), w_up[layer].astype(BF16),
                 w_down[layer].astype(BF16), norm_final, layer == depth - 1)
    return h.reshape(b, s, d)
```

```python
import functools

import numpy as np
import jax
import jax.numpy as jnp
from jax import lax
from jax.experimental import pallas as pl
from jax.experimental.pallas import tpu as pltpu

F32 = jnp.float32
BF16 = jnp.bfloat16

EPS = 1e-6
LANES = 128
HEAD_DIM = 128
ATTN_PATTERNS = ((128, 1), (512, 4), (2048, 16))
HEADS_PER_GROUP = 4
N_HEADS_A = len(ATTN_PATTERNS) * HEADS_PER_GROUP
D_ATTN = N_HEADS_A * HEAD_DIM
D_ATTN_OUT = HEADS_PER_GROUP * HEAD_DIM
ATTN_WINDOW_SUB = 128
N_HEADS_B = 8
D_DELTA = N_HEADS_B * HEAD_DIM
CONV_WIDTH = 4

VMEM_LIMIT = 56 * 1024 * 1024

TM_PROJ = 1024
PERM_ROWS = 256
TQ_ATTN = 512
DELTA_CHUNK = 128
DELTA_TILE = 512
DELTA_HEADS_PER_STEP = 4
TM_MERGE = 512
TM_FFN = 256


def _sigmoid(v):
    return 1.0 / (1.0 + jnp.exp(-v))


def _silu(v):
    return v * _sigmoid(v)


def _dot(a, b):
    return jnp.dot(a, b, preferred_element_type=F32)


def _dot_nt(a, b):
    return lax.dot_general(a, b, (((1,), (1,)), ((), ())), preferred_element_type=F32)


def _dot_tn(a, b):
    return lax.dot_general(a, b, (((0,), (0,)), ((), ())), preferred_element_type=F32)


def _norm_matmul_kernel(x_ref, g_ref, perm_ref, w_ref, o_ref, xn_ref, *, dilation):
    @pl.when(pl.program_id(1) == 0)
    def _():
        x = x_ref[...]
        ms = jnp.mean(x * x, axis=-1, keepdims=True)
        xn = (x * lax.rsqrt(ms + EPS) * g_ref[...]).astype(BF16)
        if dilation == 1:
            xn_ref[...] = xn
        else:
            rows = TM_PROJ // dilation
            piece = PERM_ROWS // dilation
            for g in range(TM_PROJ // PERM_ROWS):
                y = _dot(perm_ref[...], xn[g * PERM_ROWS:(g + 1) * PERM_ROWS, :]).astype(BF16)
                for sub in range(dilation):
                    xn_ref[sub * rows + g * piece:sub * rows + (g + 1) * piece, :] = (
                        y[sub * piece:(sub + 1) * piece, :])

    o_ref[...] = _dot(xn_ref[...], w_ref[...]).astype(o_ref.dtype).reshape(o_ref.shape)


def _stride_permutation(dilation):
    piece = PERM_ROWS // dilation
    i = np.arange(PERM_ROWS)
    src = (i % piece) * dilation + i // piece
    return jnp.asarray((src[:, None] == np.arange(PERM_ROWS)[None, :]).astype(np.float32), dtype=BF16)


def _norm_matmul(x, gain, w, out_dtype, tn, dilation=1):
    s, d = x.shape
    n = w.shape[1]
    rows = TM_PROJ // dilation
    return pl.pallas_call(
        functools.partial(_norm_matmul_kernel, dilation=dilation),
        out_shape=jax.ShapeDtypeStruct((dilation, s // dilation, n), out_dtype),
        grid=(s // TM_PROJ, n // tn),
        in_specs=[
            pl.BlockSpec((TM_PROJ, d), lambda i, j: (i, 0)),
            pl.BlockSpec((1, d), lambda i, j: (0, 0)),
            pl.BlockSpec((PERM_ROWS, PERM_ROWS), lambda i, j: (0, 0)),
            pl.BlockSpec((d, tn), lambda i, j: (0, j)),
        ],
        out_specs=pl.BlockSpec((dilation, rows, tn), lambda i, j: (0, i, j)),
        scratch_shapes=[pltpu.VMEM((TM_PROJ, d), BF16)],
        compiler_params=pltpu.CompilerParams(
            dimension_semantics=("parallel", "arbitrary"), vmem_limit_bytes=VMEM_LIMIT),
        name="norm_matmul",
    )(x, gain.reshape(1, d), _stride_permutation(dilation), w)


def _attn_bias_table(group):
    _, dilation = ATTN_PATTERNS[group]
    heads = np.arange(group * HEADS_PER_GROUP, (group + 1) * HEADS_PER_GROUP, dtype=np.float32)
    slopes = np.exp2(np.float32(-8.0) * (heads + 1) / np.float32(N_HEADS_A)).astype(np.float32)
    i = np.arange(LANES)[:, None]
    c = np.arange(2 * LANES)[None, :]
    delta = LANES + i - c
    valid = (delta >= 0) & (delta <= ATTN_WINDOW_SUB)
    bias = -(slopes * np.float32(dilation))[:, None, None] * delta.astype(np.float32)[None]
    return jnp.asarray(np.where(valid[None], bias, -np.inf).astype(np.float32))


def _attn_kernel(q_ref, kc_ref, kp_ref, vc_ref, vp_ref, bias_ref, o_ref, lse_ref):
    first_tile = pl.program_id(1) == 0
    scale = HEAD_DIM ** -0.5
    lane = lax.broadcasted_iota(jnp.int32, (LANES, LANES), 1)
    key_col = lax.broadcasted_iota(jnp.int32, (LANES, 2 * LANES), 1)
    for b in range(TQ_ATTN // LANES):
        rows = slice(b * LANES, (b + 1) * LANES)
        lse_tile = jnp.zeros((LANES, LANES), F32)
        for j in range(HEADS_PER_GROUP):
            cols = slice(j * HEAD_DIM, (j + 1) * HEAD_DIM)
            q = q_ref[rows, cols]
            if b == 0:
                k2 = jnp.concatenate([kp_ref[:, cols], kc_ref[0:LANES, cols]], axis=0)
                v2 = jnp.concatenate([vp_ref[:, cols], vc_ref[0:LANES, cols]], axis=0)
            else:
                k2 = kc_ref[(b - 1) * LANES:(b + 1) * LANES, cols]
                v2 = vc_ref[(b - 1) * LANES:(b + 1) * LANES, cols]
            s = _dot_nt(q, k2) * scale + bias_ref[j]
            if b == 0:
                s = jnp.where(jnp.logical_and(first_tile, key_col < LANES), -jnp.inf, s)
            m = jnp.max(s, axis=-1, keepdims=True)
            p = jnp.exp(s - m)
            den = jnp.sum(p, axis=-1, keepdims=True)
            o_ref[rows, cols] = (_dot(p.astype(BF16), v2) / den).astype(o_ref.dtype)
            lse_tile = jnp.where(lane == j, m + jnp.log(den), lse_tile)
        lse_ref[rows, :] = lse_tile


def _attn_group(qkv, group):
    d, r, _ = qkv.shape
    s = d * r
    halo_step = TQ_ATTN // LANES

    def cur(section):
        return pl.BlockSpec((None, TQ_ATTN, D_ATTN_OUT), lambda sub, n: (sub, n, section))

    def prev(section):
        return pl.BlockSpec((None, LANES, D_ATTN_OUT),
                            lambda sub, n: (sub, jnp.maximum(n * halo_step - 1, 0), section))

    o, lse = pl.pallas_call(
        _attn_kernel,
        out_shape=(jax.ShapeDtypeStruct((r, d * D_ATTN_OUT), BF16),
                   jax.ShapeDtypeStruct((r, d * LANES), F32)),
        grid=(d, r // TQ_ATTN),
        in_specs=[cur(0), cur(1), prev(1), cur(2), prev(2),
                  pl.BlockSpec((HEADS_PER_GROUP, LANES, 2 * LANES), lambda sub, n: (0, 0, 0))],
        out_specs=(pl.BlockSpec((TQ_ATTN, D_ATTN_OUT), lambda sub, n: (n, sub)),
                   pl.BlockSpec((TQ_ATTN, LANES), lambda sub, n: (n, sub))),
        compiler_params=pltpu.CompilerParams(
            dimension_semantics=("parallel", "arbitrary"), vmem_limit_bytes=VMEM_LIMIT),
        name=f"attn_group{group}",
    )(qkv, qkv, qkv, qkv, qkv, _attn_bias_table(group))
    return o.reshape(s, D_ATTN_OUT), lse.reshape(s, LANES)


def _delta_kernel(qx_ref, kx_ref, vx_ref, qh_ref, kh_ref, vh_ref, z_ref, ba_ref,
                  cwq_ref, cwk_ref, cwv_ref, alog_ref, dtb_ref, dn_ref, tri_ref, out_ref,
                  state_ref, kn_ref, kbeta_ref, qn_ref, qd_ref, kd_ref, vb_ref, kbe_ref,
                  gc_ref, gl_ref, gct_ref, o_ref, b_ref, inv_ref, m_ref, qk_ref, ubar_ref, w_ref):
    c_len = DELTA_CHUNK
    n_heads = DELTA_HEADS_PER_STEP
    tile = pl.program_id(1)

    @pl.when(tile == 0)
    def _():
        state_ref[...] = jnp.zeros_like(state_ref)

    def conv_silu(x_ref, halo_ref, w_ref):
        x = x_ref[...]
        halo = jnp.where(tile == 0, 0.0, halo_ref[...])
        w = w_ref[...]
        row8 = lax.broadcasted_iota(jnp.int32, halo.shape, 0)
        y = w[CONV_WIDTH - 1:CONV_WIDTH, :] * x
        for s in range(1, CONV_WIDTH):
            xs = pltpu.roll(x, s, 0)
            head = jnp.where(row8 < s, pltpu.roll(halo, s, 0), xs[0:8])
            xs = jnp.concatenate([head, xs[8:]], axis=0)
            y = y + w[CONV_WIDTH - 1 - s:CONV_WIDTH - s, :] * xs
        return _silu(y)

    def l2n(v):
        return v * lax.rsqrt(jnp.sum(v * v, axis=-1, keepdims=True) + EPS)

    ba = ba_ref[...]
    lane = lax.broadcasted_iota(jnp.int32, ba.shape, 1)
    pre = ba + dtb_ref[...]
    softplus = jnp.maximum(pre, 0.0) + jnp.log(1.0 + jnp.exp(-jnp.abs(pre)))
    bg = jnp.where(lane < 8, _sigmoid(ba), -jnp.exp(alog_ref[...]) * softplus)
    hi = bg.astype(BF16)
    r1 = bg - hi.astype(F32)
    mid = r1.astype(BF16)
    lo = (r1 - mid.astype(F32)).astype(BF16)
    sums = _dot(tri_ref[...], jnp.concatenate([hi, mid, lo], axis=1))
    sums = sums[:, 0:LANES] + sums[:, LANES:2 * LANES] + sums[:, 2 * LANES:3 * LANES]
    gc = sums[0:DELTA_TILE]
    gl = sums[DELTA_TILE:2 * DELTA_TILE]
    gc_ref[...] = gc
    gl_ref[...] = gl
    for c in range(DELTA_TILE // c_len):
        gct_ref[c] = jnp.transpose(gc[c * c_len:(c + 1) * c_len, :])

    q_all = conv_silu(qx_ref, qh_ref, cwq_ref)
    for h in range(n_heads):
        cols = slice(h * HEAD_DIM, (h + 1) * HEAD_DIM)
        qn = l2n(q_all[:, cols]) * (HEAD_DIM ** -0.5)
        qn_ref[:, cols] = qn.astype(BF16)
        qd_ref[:, cols] = (qn * jnp.exp(gc[:, 8 + h:9 + h])).astype(BF16)
    k_all = conv_silu(kx_ref, kh_ref, cwk_ref)
    for h in range(n_heads):
        cols = slice(h * HEAD_DIM, (h + 1) * HEAD_DIM)
        kn = l2n(k_all[:, cols])
        beta = bg[:, h:h + 1]
        gch = gc[:, 8 + h:9 + h]
        kn_ref[:, cols] = kn.astype(BF16)
        kbeta_ref[:, cols] = (kn * beta).astype(BF16)
        kbe_ref[:, cols] = (kn * (beta * jnp.exp(gch))).astype(BF16)
        kd_ref[:, cols] = (kn * jnp.exp(gl[:, 8 + h:9 + h] - gch)).astype(BF16)
    v_all = conv_silu(vx_ref, vh_ref, cwv_ref)
    for h in range(n_heads):
        cols = slice(h * HEAD_DIM, (h + 1) * HEAD_DIM)
        vb_ref[:, cols] = (v_all[:, cols] * bg[:, h:h + 1]).astype(BF16)

    row = lax.broadcasted_iota(jnp.int32, (c_len, c_len), 0)
    col = lax.broadcasted_iota(jnp.int32, (c_len, c_len), 1)
    lower = row >= col
    strict = row > col
    eye = jnp.where(row == col, 1.0, 0.0).astype(F32)
    pair = jnp.right_shift(row, 1) == jnp.right_shift(col, 1)
    n_levels = c_len.bit_length() - 2

    n_chunks = DELTA_TILE // c_len
    items = [(c, h) for c in range(n_chunks) for h in range(n_heads)]

    def rows_of(c):
        return slice(c * c_len, (c + 1) * c_len)

    def cols_of(h):
        return slice(h * HEAD_DIM, (h + 1) * HEAD_DIM)

    for i, (c, h) in enumerate(items):
        r, cols = rows_of(c), cols_of(h)
        kq = _dot_nt(jnp.concatenate([kbeta_ref[r, cols], qn_ref[r, cols]], axis=0), kn_ref[r, cols])
        g_col = gc_ref[r, 8 + h:9 + h]
        g_row = gct_ref[c, 8 + h:9 + h, :]
        decay = jnp.exp(jnp.where(lower, g_col - g_row, -jnp.inf))
        p = jnp.where(strict, -(kq[0:c_len] * decay), 0.0)
        b_ref[i] = p.astype(BF16)
        inv_ref[i] = (eye + jnp.where(pair, p, 0.0)).astype(BF16)
        qk_ref[i] = (kq[c_len:2 * c_len] * decay).astype(BF16)

    def level(l, carry):
        lb = l + 1
        off = jnp.logical_and(jnp.right_shift(row, lb) == jnp.right_shift(col, lb) + 1,
                              jnp.bitwise_and(jnp.right_shift(col, lb), 1) == 0)
        for i in range(len(items)):
            m_ref[i] = jnp.where(off, _dot(b_ref[i], inv_ref[i]), 0.0).astype(BF16)
        for i in range(len(items)):
            d = inv_ref[i]
            inv_ref[i] = d + _dot(d, m_ref[i]).astype(BF16)
        return carry

    lax.fori_loop(0, n_levels, level, 0)

    for i, (c, h) in enumerate(items):
        r, cols = rows_of(c), cols_of(h)
        sol = _dot(inv_ref[i], jnp.concatenate([vb_ref[r, cols], kbe_ref[r, cols]], axis=1))
        ubar_ref[r, cols] = sol[:, 0:HEAD_DIM]
        w_ref[r, cols] = sol[:, HEAD_DIM:2 * HEAD_DIM].astype(BF16)

    for c in range(n_chunks):
        r = rows_of(c)
        ws = [_dot(jnp.concatenate([w_ref[r, cols_of(h)], qd_ref[r, cols_of(h)]], axis=0),
                   state_ref[h].astype(BF16)) for h in range(n_heads)]
        us = [(ubar_ref[r, cols_of(h)] - ws[h][0:c_len]).astype(BF16) for h in range(n_heads)]
        for h in range(n_heads):
            cols = cols_of(h)
            o_ref[r, cols] = ws[h][c_len:2 * c_len] + _dot(qk_ref[c * n_heads + h], us[h])
            g_last = jnp.exp(jnp.broadcast_to(gl_ref[r, 8 + h:9 + h], (c_len, HEAD_DIM))[0:1, :])
            state_ref[h] = (jnp.broadcast_to(g_last, (HEAD_DIM, HEAD_DIM)) * state_ref[h]
                            + _dot_tn(kd_ref[r, cols], us[h]))

    dn = dn_ref[...]
    for h in range(n_heads):
        cols = slice(h * HEAD_DIM, (h + 1) * HEAD_DIM)
        o = o_ref[:, cols]
        o = o * lax.rsqrt(jnp.mean(o * o, axis=-1, keepdims=True) + EPS) * dn
        out_ref[:, cols] = o * _silu(z_ref[:, cols])


def _delta_tri_matrix():
    i = np.arange(DELTA_TILE)[:, None]
    t = np.arange(DELTA_TILE)[None, :]
    same = (i // DELTA_CHUNK) == (t // DELTA_CHUNK)
    return jnp.asarray(np.concatenate([same & (t <= i), same], axis=0).astype(np.float32), dtype=BF16)


def _delta_rule(rest, ba, conv_w, a_log, dt_bias, dn_norm):
    s = rest.shape[0]
    hps = DELTA_HEADS_PER_STEP
    width = hps * HEAD_DIM
    groups = N_HEADS_B // hps
    per_section = D_DELTA // width
    halo_blocks = DELTA_TILE // 8
    n_items = (DELTA_TILE // DELTA_CHUNK) * hps

    def main(section):
        return pl.BlockSpec((DELTA_TILE, width), lambda g, t: (t, section * per_section + g))

    def halo(section):
        return pl.BlockSpec((8, width), lambda g, t: (jnp.maximum(t * halo_blocks - 1, 0),
                                                      section * per_section + g))

    def cw(section):
        return pl.BlockSpec((CONV_WIDTH, width), lambda g, t: (0, section * per_section + g))

    def lane_table(v):
        tab = jnp.zeros((groups, LANES), F32).at[:, 8:8 + hps].set(v.reshape(groups, hps).astype(F32))
        return tab.reshape(1, groups * LANES)

    bf = lambda: pltpu.VMEM((DELTA_TILE, width), BF16)
    return pl.pallas_call(
        _delta_kernel,
        out_shape=jax.ShapeDtypeStruct((s, D_DELTA), F32),
        grid=(groups, s // DELTA_TILE),
        in_specs=[main(0), main(1), main(2), halo(0), halo(1), halo(2), main(3),
                  pl.BlockSpec((DELTA_TILE, LANES), lambda g, t: (t, g)),
                  cw(0), cw(1), cw(2),
                  pl.BlockSpec((1, LANES), lambda g, t: (0, g)),
                  pl.BlockSpec((1, LANES), lambda g, t: (0, g)),
                  pl.BlockSpec((1, HEAD_DIM), lambda g, t: (0, 0)),
                  pl.BlockSpec((2 * DELTA_TILE, DELTA_TILE), lambda g, t: (0, 0))],
        out_specs=pl.BlockSpec((DELTA_TILE, width), lambda g, t: (t, g)),
        scratch_shapes=[
            pltpu.VMEM((hps, HEAD_DIM, HEAD_DIM), F32),
            bf(), bf(), bf(), bf(), bf(), bf(), bf(),
            pltpu.VMEM((DELTA_TILE, LANES), F32),
            pltpu.VMEM((DELTA_TILE, LANES), F32),
            pltpu.VMEM((DELTA_TILE // DELTA_CHUNK, LANES, DELTA_CHUNK), F32),
            pltpu.VMEM((DELTA_TILE, width), F32),
            pltpu.VMEM((n_items, DELTA_CHUNK, DELTA_CHUNK), BF16),
            pltpu.VMEM((n_items, DELTA_CHUNK, DELTA_CHUNK), BF16),
            pltpu.VMEM((n_items, DELTA_CHUNK, DELTA_CHUNK), BF16),
            pltpu.VMEM((n_items, DELTA_CHUNK, DELTA_CHUNK), BF16),
            pltpu.VMEM((DELTA_TILE, width), F32),
            pltpu.VMEM((DELTA_TILE, width), BF16),
        ],
        compiler_params=pltpu.CompilerParams(
            dimension_semantics=("parallel", "arbitrary"), vmem_limit_bytes=VMEM_LIMIT),
        name="delta_rule",
    )(rest, rest, rest, rest, rest, rest, rest, ba, conv_w, conv_w, conv_w,
      lane_table(a_log), lane_table(dt_bias), dn_norm.reshape(1, HEAD_DIM).astype(F32),
      _delta_tri_matrix())


def _merge_kernel(o0_ref, o1_ref, o2_ref, l0_ref, l1_ref, l2_ref, od_ref, ga_ref, gb_ref, x_ref,
                  wpa_ref, wpd_ref, wo_ref, out_ref):
    l0, l1, l2 = l0_ref[...], l1_ref[...], l2_ref[...]
    mx = jnp.maximum(jnp.maximum(l0, l1), l2)
    e0, e1, e2 = jnp.exp(l0 - mx), jnp.exp(l1 - mx), jnp.exp(l2 - mx)
    inv = 1.0 / (e0 + e1 + e2)
    w0, w1, w2 = e0 * inv, e1 * inv, e2 * inv
    parts = []
    for j in range(HEADS_PER_GROUP):
        cols = slice(j * HEAD_DIM, (j + 1) * HEAD_DIM)
        parts.append(w0[:, j:j + 1] * o0_ref[:, cols].astype(F32) + w1[:, j:j + 1] * o1_ref[:, cols].astype(F32)
                     + w2[:, j:j + 1] * o2_ref[:, cols].astype(F32))
    y_a = _dot(jnp.concatenate(parts, axis=1).astype(BF16), wpa_ref[...])
    y_b = _dot(od_ref[...].astype(BF16), wpd_ref[...])
    merged = _sigmoid(ga_ref[...]) * y_a + _sigmoid(gb_ref[...]) * y_b
    out_ref[...] = x_ref[...] + _dot(merged.astype(BF16), wo_ref[...])


def _merge_proj(outs, lses, o_delta, rest, x, w_pa, w_pd, w_o):
    s, d = x.shape
    tm = TM_MERGE
    gate_block = (3 * D_DELTA + D_DELTA) // d
    row = lambda width: pl.BlockSpec((tm, width), lambda i: (i, 0))
    full = lambda a: pl.BlockSpec(a.shape, lambda i: (0, 0))
    return pl.pallas_call(
        _merge_kernel,
        out_shape=jax.ShapeDtypeStruct((s, d), F32),
        grid=(s // tm,),
        in_specs=[row(D_ATTN_OUT)] * 3 + [row(LANES)] * 3 + [
            row(D_DELTA),
            pl.BlockSpec((tm, d), lambda i: (i, gate_block)),
            pl.BlockSpec((tm, d), lambda i: (i, gate_block + 1)),
            row(d), full(w_pa), full(w_pd), full(w_o)],
        out_specs=row(d),
        compiler_params=pltpu.CompilerParams(
            dimension_semantics=("parallel",), vmem_limit_bytes=VMEM_LIMIT),
        name="merge_proj",
    )(*outs, *lses, o_delta, rest, rest, x, w_pa, w_pd, w_o)


def _ffn_kernel(h_ref, g_ref, wg_ref, wu_ref, wd_ref, gf_ref, o_ref, *, final_norm):
    h = h_ref[...]
    hn = (h * lax.rsqrt(jnp.mean(h * h, axis=-1, keepdims=True) + EPS) * g_ref[...]).astype(BF16)
    act = (_silu(_dot(hn, wg_ref[...])) * _dot(hn, wu_ref[...])).astype(BF16)
    y = h + _dot(act, wd_ref[...])
    if final_norm:
        y = y * lax.rsqrt(jnp.mean(y * y, axis=-1, keepdims=True) + EPS) * gf_ref[...]
    o_ref[...] = y


def _ffn(h, gain, w_g, w_u, w_d, gain_final, final_norm):
    s, d = h.shape
    tm = TM_FFN
    full = lambda a: pl.BlockSpec(a.shape, lambda i: (0, 0))
    vec = pl.BlockSpec((1, d), lambda i: (0, 0))
    return pl.pallas_call(
        functools.partial(_ffn_kernel, final_norm=final_norm),
        out_shape=jax.ShapeDtypeStruct((s, d), F32),
        grid=(s // tm,),
        in_specs=[pl.BlockSpec((tm, d), lambda i: (i, 0)), vec, full(w_g), full(w_u), full(w_d), vec],
        out_specs=pl.BlockSpec((tm, d), lambda i: (i, 0)),
        compiler_params=pltpu.CompilerParams(
            dimension_semantics=("parallel",), vmem_limit_bytes=VMEM_LIMIT),
        name="ffn",
    )(h, gain.reshape(1, d), w_g, w_u, w_d, gain_final.reshape(1, d))


def _gate_weight_columns(w_in, col_beta, col_a):
    hps = DELTA_HEADS_PER_STEP
    groups = N_HEADS_B // hps
    d = w_in.shape[0]
    w = jnp.zeros((d, groups, LANES), w_in.dtype)
    w = w.at[:, :, 0:hps].set(w_in[:, col_beta:col_beta + N_HEADS_B].reshape(d, groups, hps))
    w = w.at[:, :, 8:8 + hps].set(w_in[:, col_a:col_a + N_HEADS_B].reshape(d, groups, hps))
    return w.reshape(d, groups * LANES)


def kernel(x, norm_mix, w_in, conv_w, a_log, dt_bias, dn_norm, w_proj_attn, w_proj_delta,
           w_out, norm_ffn, w_gate, w_up, w_down, norm_final):
    b, s, d = x.shape
    assert b == 1 and s % (TM_PROJ * 16) == 0 and d == 8 * LANES
    depth = w_in.shape[0]
    c_attn = 3 * D_ATTN
    c_z = c_attn + 4 * D_DELTA
    c_beta, c_a = c_z, c_z + N_HEADS_B
    c_gate = c_z + 2 * N_HEADS_B
    h = x.reshape(s, d)
    for layer in range(depth):
        wl = w_in[layer]
        w_rest = jnp.concatenate([wl[:, c_attn:c_z], wl[:, c_gate:]], axis=1).astype(BF16)
        w_ba = _gate_weight_columns(wl, c_beta, c_a).astype(BF16)
        rest = _norm_matmul(h, norm_mix[layer], w_rest, F32, 1536)[0]
        ba = _norm_matmul(h, norm_mix[layer], w_ba, F32, w_ba.shape[1])[0]

        outs, lses = [], []
        for g, (_, dilation) in enumerate(ATTN_PATTERNS):
            w_g = jnp.concatenate([wl[:, sec * D_ATTN + g * D_ATTN_OUT:sec * D_ATTN + (g + 1) * D_ATTN_OUT]
                                   for sec in range(3)], axis=1).astype(BF16)
            qkv_g = _norm_matmul(h, norm_mix[layer], w_g, BF16, 3 * D_ATTN_OUT, dilation)
            o_g, lse_g = _attn_group(qkv_g, g)
            outs.append(o_g)
            lses.append(lse_g)
        o_delta = _delta_rule(rest, ba, conv_w[layer], a_log[layer], dt_bias[layer], dn_norm[layer])
        h = _merge_proj(outs, lses, o_delta, rest, h, w_proj_attn[layer].astype(BF16),
                        w_proj_delta[layer].astype(BF16), w_out[layer].astype(BF16))
        h = _ffn(h, norm_ffn[layer], w_gate[layer].astype(BF16), w_up[layer].astype(BF16),
                 w_down[layer].astype(BF16), norm_final, layer == depth - 1)
    return h.reshape(b, s, d)
```

```python
import functools

import numpy as np
import jax
import jax.numpy as jnp
from jax import lax
from jax.experimental import pallas as pl
from jax.experimental.pallas import tpu as pltpu

F32 = jnp.float32
BF16 = jnp.bfloat16

EPS = 1e-6
LANES = 128
HEAD_DIM = 128
ATTN_PATTERNS = ((128, 1), (512, 4), (2048, 16))
HEADS_PER_GROUP = 4
N_HEADS_A = len(ATTN_PATTERNS) * HEADS_PER_GROUP
D_ATTN = N_HEADS_A * HEAD_DIM
D_ATTN_OUT = HEADS_PER_GROUP * HEAD_DIM
ATTN_WINDOW_SUB = 128
N_HEADS_B = 8
D_DELTA = N_HEADS_B * HEAD_DIM
CONV_WIDTH = 4

VMEM_LIMIT = 56 * 1024 * 1024

TM_PROJ = 1024
CONV_HALO = 16
PROJ_CHUNK = 256
PERM_ROWS = 256
TQ_ATTN = 512
DELTA_CHUNK = 128
DELTA_TILE = 512
DELTA_HEADS_PER_STEP = 4
TM_MERGE = 512
TM_FFN = 256


def _sigmoid(v):
    return 1.0 / (1.0 + jnp.exp(-v))


def _silu(v):
    return v * _sigmoid(v)


def _dot(a, b):
    return jnp.dot(a, b, preferred_element_type=F32)


def _dot_nt(a, b):
    return lax.dot_general(a, b, (((1,), (1,)), ((), ())), preferred_element_type=F32)


def _dot_tn(a, b):
    return lax.dot_general(a, b, (((0,), (0,)), ((), ())), preferred_element_type=F32)


def _norm_matmul_kernel(x_ref, g_ref, perm_ref, w_ref, o_ref, xn_ref, *, dilation):
    @pl.when(pl.program_id(1) == 0)
    def _():
        x = x_ref[...]
        ms = jnp.mean(x * x, axis=-1, keepdims=True)
        xn = (x * lax.rsqrt(ms + EPS) * g_ref[...]).astype(BF16)
        if dilation == 1:
            xn_ref[...] = xn
        else:
            rows = TM_PROJ // dilation
            piece = PERM_ROWS // dilation
            for g in range(TM_PROJ // PERM_ROWS):
                y = _dot(perm_ref[...], xn[g * PERM_ROWS:(g + 1) * PERM_ROWS, :]).astype(BF16)
                for sub in range(dilation):
                    xn_ref[sub * rows + g * piece:sub * rows + (g + 1) * piece, :] = (
                        y[sub * piece:(sub + 1) * piece, :])

    o_ref[...] = _dot(xn_ref[...], w_ref[...]).astype(o_ref.dtype).reshape(o_ref.shape)


def _stride_permutation(dilation):
    piece = PERM_ROWS // dilation
    i = np.arange(PERM_ROWS)
    src = (i % piece) * dilation + i // piece
    return jnp.asarray((src[:, None] == np.arange(PERM_ROWS)[None, :]).astype(np.float32), dtype=BF16)


def _norm_matmul(x, gain, w, out_dtype, tn, dilation=1):
    s, d = x.shape
    n = w.shape[1]
    rows = TM_PROJ // dilation
    return pl.pallas_call(
        functools.partial(_norm_matmul_kernel, dilation=dilation),
        out_shape=jax.ShapeDtypeStruct((dilation, s // dilation, n), out_dtype),
        grid=(s // TM_PROJ, n // tn),
        in_specs=[
            pl.BlockSpec((TM_PROJ, d), lambda i, j: (i, 0)),
            pl.BlockSpec((1, d), lambda i, j: (0, 0)),
            pl.BlockSpec((PERM_ROWS, PERM_ROWS), lambda i, j: (0, 0)),
            pl.BlockSpec((d, tn), lambda i, j: (0, j)),
        ],
        out_specs=pl.BlockSpec((dilation, rows, tn), lambda i, j: (0, i, j)),
        scratch_shapes=[pltpu.VMEM((TM_PROJ, d), BF16)],
        compiler_params=pltpu.CompilerParams(
            dimension_semantics=("parallel", "arbitrary"), vmem_limit_bytes=VMEM_LIMIT),
        name="norm_matmul",
    )(x, gain.reshape(1, d), _stride_permutation(dilation), w)


def _proj_act_kernel(*refs, conv, l2norm, act, with_gates, q_tiles):
    refs = list(refs)
    x_ref, g_ref, w_ref = refs[0:3]
    pos = 3
    if conv:
        xh_ref, cw_ref = refs[pos:pos + 2]
        pos += 2
    if with_gates:
        wba_ref = refs[pos]
        pos += 1
    o_ref = refs[pos]
    pos += 1
    if with_gates:
        ba_ref = refs[pos]
        pos += 1
    xn_ref = refs[pos]
    i, j = pl.program_id(0), pl.program_id(1)
    tm = TM_PROJ

    def norm(x):
        return (x * lax.rsqrt(jnp.mean(x * x, axis=-1, keepdims=True) + EPS) * g_ref[...]).astype(BF16)

    @pl.when(j == 0)
    def _():
        xn_ref[0:tm, :] = norm(x_ref[...])
        if conv:
            xn_ref[tm:tm + CONV_HALO, :] = jnp.where(i == 0, jnp.zeros((), BF16), norm(xh_ref[...]))
        if with_gates:
            ba_ref[...] = _dot(xn_ref[0:tm, :], wba_ref[...])

    def epilogue(acc, c0):
        width = acc.shape[1]
        if conv:
            x = acc[0:tm]
            halo = acc[tm + CONV_HALO - 8:tm + CONV_HALO]
            w = cw_ref[:, c0:c0 + width]
            row8 = lax.broadcasted_iota(jnp.int32, halo.shape, 0)
            y = w[CONV_WIDTH - 1:CONV_WIDTH, :] * x
            for s in range(1, CONV_WIDTH):
                xs = pltpu.roll(x, s, 0)
                head = jnp.where(row8 < s, pltpu.roll(halo, s, 0), xs[0:8])
                xs = jnp.concatenate([head, xs[8:]], axis=0)
                y = y + w[CONV_WIDTH - 1 - s:CONV_WIDTH - s, :] * xs
        else:
            y = acc
        sig = _sigmoid(y)
        if act == "silu":
            y = y * sig
        elif act == "sigmoid":
            y = sig
        else:
            y = sig * jnp.where(j == 0, y, 1.0)
        if l2norm:
            scale = jnp.where(j < q_tiles, HEAD_DIM ** -0.5, 1.0)
            for h in range(width // HEAD_DIM):
                seg = y[:, h * HEAD_DIM:(h + 1) * HEAD_DIM]
                inv = lax.rsqrt(jnp.sum(seg * seg, axis=-1, keepdims=True) + EPS) * scale
                o_ref[:, c0 + h * HEAD_DIM:c0 + (h + 1) * HEAD_DIM] = (seg * inv).astype(o_ref.dtype)
        else:
            o_ref[:, c0:c0 + width] = y.astype(o_ref.dtype)

    n_chunks = o_ref.shape[1] // PROJ_CHUNK
    pending = None
    for n in range(n_chunks):
        acc = _dot(xn_ref[...], w_ref[:, n * PROJ_CHUNK:(n + 1) * PROJ_CHUNK])
        if pending is not None:
            epilogue(pending, (n - 1) * PROJ_CHUNK)
        pending = acc
    epilogue(pending, (n_chunks - 1) * PROJ_CHUNK)


def _proj_act(x, gain, w, tn, *, conv_w=None, l2norm=False, act="silu", w_gates=None, q_tiles=0):
    s, d = x.shape
    n = w.shape[1]
    conv = conv_w is not None
    with_gates = w_gates is not None
    halo_blocks = TM_PROJ // CONV_HALO
    in_specs = [pl.BlockSpec((TM_PROJ, d), lambda i, j: (i, 0)),
                pl.BlockSpec((1, d), lambda i, j: (0, 0)),
                pl.BlockSpec((d, tn), lambda i, j: (0, j))]
    args = [x, gain.reshape(1, d), w]
    if conv:
        in_specs += [pl.BlockSpec((CONV_HALO, d), lambda i, j: (jnp.maximum(i * halo_blocks - 1, 0), 0)),
                     pl.BlockSpec((CONV_WIDTH, tn), lambda i, j: (0, j))]
        args += [x, conv_w]
    out_shape = [jax.ShapeDtypeStruct((s, n), BF16)]
    out_specs = [pl.BlockSpec((TM_PROJ, tn), lambda i, j: (i, j))]
    if with_gates:
        ng = w_gates.shape[1]
        in_specs.append(pl.BlockSpec((d, ng), lambda i, j: (0, 0)))
        args.append(w_gates)
        out_shape.append(jax.ShapeDtypeStruct((s, ng), F32))
        out_specs.append(pl.BlockSpec((TM_PROJ, ng), lambda i, j: (i, 0)))
    res = pl.pallas_call(
        functools.partial(_proj_act_kernel, conv=conv, l2norm=l2norm, act=act, with_gates=with_gates,
                          q_tiles=q_tiles),
        out_shape=out_shape,
        grid=(s // TM_PROJ, n // tn),
        in_specs=in_specs,
        out_specs=out_specs,
        scratch_shapes=[pltpu.VMEM((TM_PROJ + (CONV_HALO if conv else 0), d), BF16)],
        compiler_params=pltpu.CompilerParams(
            dimension_semantics=("parallel", "arbitrary"), vmem_limit_bytes=VMEM_LIMIT),
        name="proj_act",
    )(*args)
    return res if with_gates else res[0]


def _attn_bias_table(group):
    _, dilation = ATTN_PATTERNS[group]
    heads = np.arange(group * HEADS_PER_GROUP, (group + 1) * HEADS_PER_GROUP, dtype=np.float32)
    slopes = np.exp2(np.float32(-8.0) * (heads + 1) / np.float32(N_HEADS_A)).astype(np.float32)
    i = np.arange(LANES)[:, None]
    c = np.arange(2 * LANES)[None, :]
    delta = LANES + i - c
    valid = (delta >= 0) & (delta <= ATTN_WINDOW_SUB)
    bias = -(slopes * np.float32(dilation))[:, None, None] * delta.astype(np.float32)[None]
    return jnp.asarray(np.where(valid[None], bias, -np.inf).astype(np.float32))


def _attn_kernel(q_ref, kc_ref, kp_ref, vc_ref, vp_ref, bias_ref, o_ref, lse_ref):
    first_tile = pl.program_id(1) == 0
    scale = HEAD_DIM ** -0.5
    lane = lax.broadcasted_iota(jnp.int32, (LANES, LANES), 1)
    key_col = lax.broadcasted_iota(jnp.int32, (LANES, 2 * LANES), 1)
    for b in range(TQ_ATTN // LANES):
        rows = slice(b * LANES, (b + 1) * LANES)
        lse_tile = jnp.zeros((LANES, LANES), F32)
        for j in range(HEADS_PER_GROUP):
            cols = slice(j * HEAD_DIM, (j + 1) * HEAD_DIM)
            q = q_ref[rows, cols]
            if b == 0:
                k2 = jnp.concatenate([kp_ref[:, cols], kc_ref[0:LANES, cols]], axis=0)
                v2 = jnp.concatenate([vp_ref[:, cols], vc_ref[0:LANES, cols]], axis=0)
            else:
                k2 = kc_ref[(b - 1) * LANES:(b + 1) * LANES, cols]
                v2 = vc_ref[(b - 1) * LANES:(b + 1) * LANES, cols]
            s = _dot_nt(q, k2) * scale + bias_ref[j]
            if b == 0:
                s = jnp.where(jnp.logical_and(first_tile, key_col < LANES), -jnp.inf, s)
            m = jnp.max(s, axis=-1, keepdims=True)
            p = jnp.exp(s - m)
            den = jnp.sum(p, axis=-1, keepdims=True)
            o_ref[rows, cols] = (_dot(p.astype(BF16), v2) / den).astype(o_ref.dtype)
            lse_tile = jnp.where(lane == j, m + jnp.log(den), lse_tile)
        lse_ref[rows, :] = lse_tile


def _attn_group(qkv, group):
    d, r, _ = qkv.shape
    s = d * r
    halo_step = TQ_ATTN // LANES

    def cur(section):
        return pl.BlockSpec((None, TQ_ATTN, D_ATTN_OUT), lambda sub, n: (sub, n, section))

    def prev(section):
        return pl.BlockSpec((None, LANES, D_ATTN_OUT),
                            lambda sub, n: (sub, jnp.maximum(n * halo_step - 1, 0), section))

    o, lse = pl.pallas_call(
        _attn_kernel,
        out_shape=(jax.ShapeDtypeStruct((r, d * D_ATTN_OUT), BF16),
                   jax.ShapeDtypeStruct((r, d * LANES), F32)),
        grid=(d, r // TQ_ATTN),
        in_specs=[cur(0), cur(1), prev(1), cur(2), prev(2),
                  pl.BlockSpec((HEADS_PER_GROUP, LANES, 2 * LANES), lambda sub, n: (0, 0, 0))],
        out_specs=(pl.BlockSpec((TQ_ATTN, D_ATTN_OUT), lambda sub, n: (n, sub)),
                   pl.BlockSpec((TQ_ATTN, LANES), lambda sub, n: (n, sub))),
        compiler_params=pltpu.CompilerParams(
            dimension_semantics=("parallel", "arbitrary"), vmem_limit_bytes=VMEM_LIMIT),
        name=f"attn_group{group}",
    )(qkv, qkv, qkv, qkv, qkv, _attn_bias_table(group))
    return o.reshape(s, D_ATTN_OUT), lse.reshape(s, LANES)


def _delta_kernel(q_ref, k_ref, v_ref, sz_ref, ba_ref, alog_ref, dtb_ref, dn_ref, tri_ref, out_ref,
                  state_ref, kbeta_ref, qd_ref, kd_ref, vb_ref, kbe_ref,
                  gc_ref, gl_ref, gct_ref, o_ref, b_ref, inv_ref, m_ref, qk_ref, ubar_ref, w_ref):
    c_len = DELTA_CHUNK
    n_heads = DELTA_HEADS_PER_STEP
    tile = pl.program_id(1)

    @pl.when(tile == 0)
    def _():
        state_ref[...] = jnp.zeros_like(state_ref)

    ba = ba_ref[...]
    lane = lax.broadcasted_iota(jnp.int32, ba.shape, 1)
    pre = ba + dtb_ref[...]
    softplus = jnp.maximum(pre, 0.0) + jnp.log(1.0 + jnp.exp(-jnp.abs(pre)))
    bg = jnp.where(lane < 8, _sigmoid(ba), -jnp.exp(alog_ref[...]) * softplus)
    hi = bg.astype(BF16)
    r1 = bg - hi.astype(F32)
    mid = r1.astype(BF16)
    lo = (r1 - mid.astype(F32)).astype(BF16)
    sums = _dot(tri_ref[...], jnp.concatenate([hi, mid, lo], axis=1))
    sums = sums[:, 0:LANES] + sums[:, LANES:2 * LANES] + sums[:, 2 * LANES:3 * LANES]
    gc = sums[0:DELTA_TILE]
    gl = sums[DELTA_TILE:2 * DELTA_TILE]
    gc_ref[...] = gc
    gl_ref[...] = gl
    for c in range(DELTA_TILE // c_len):
        gct_ref[c] = jnp.transpose(gc[c * c_len:(c + 1) * c_len, :])

    for h in range(n_heads):
        cols = slice(h * HEAD_DIM, (h + 1) * HEAD_DIM)
        beta = bg[:, h:h + 1]
        gch = gc[:, 8 + h:9 + h]
        e_gc = jnp.exp(gch)
        kn = k_ref[:, cols].astype(F32)
        qd_ref[:, cols] = (q_ref[:, cols].astype(F32) * e_gc).astype(BF16)
        kbeta_ref[:, cols] = (kn * beta).astype(BF16)
        kbe_ref[:, cols] = (kn * (beta * e_gc)).astype(BF16)
        kd_ref[:, cols] = (kn * jnp.exp(gl[:, 8 + h:9 + h] - gch)).astype(BF16)
        vb_ref[:, cols] = (v_ref[:, cols].astype(F32) * beta).astype(BF16)

    row = lax.broadcasted_iota(jnp.int32, (c_len, c_len), 0)
    col = lax.broadcasted_iota(jnp.int32, (c_len, c_len), 1)
    lower = row >= col
    strict = row > col
    eye = jnp.where(row == col, 1.0, 0.0).astype(F32)
    pair = jnp.right_shift(row, 1) == jnp.right_shift(col, 1)
    n_levels = c_len.bit_length() - 2

    n_chunks = DELTA_TILE // c_len
    items = [(c, h) for c in range(n_chunks) for h in range(n_heads)]

    def rows_of(c):
        return slice(c * c_len, (c + 1) * c_len)

    def cols_of(h):
        return slice(h * HEAD_DIM, (h + 1) * HEAD_DIM)

    for i, (c, h) in enumerate(items):
        r, cols = rows_of(c), cols_of(h)
        kq = _dot_nt(jnp.concatenate([kbeta_ref[r, cols], q_ref[r, cols]], axis=0), k_ref[r, cols])
        g_col = gc_ref[r, 8 + h:9 + h]
        g_row = gct_ref[c, 8 + h:9 + h, :]
        decay = jnp.exp(jnp.where(lower, g_col - g_row, -jnp.inf))
        p = jnp.where(strict, -(kq[0:c_len] * decay), 0.0)
        b_ref[i] = p.astype(BF16)
        inv_ref[i] = (eye + jnp.where(pair, p, 0.0)).astype(BF16)
        qk_ref[i] = (kq[c_len:2 * c_len] * decay).astype(BF16)

    def level(l, carry):
        lb = l + 1
        off = jnp.logical_and(jnp.right_shift(row, lb) == jnp.right_shift(col, lb) + 1,
                              jnp.bitwise_and(jnp.right_shift(col, lb), 1) == 0)
        for i in range(len(items)):
            m_ref[i] = jnp.where(off, _dot(b_ref[i], inv_ref[i]), 0.0).astype(BF16)
        for i in range(len(items)):
            d = inv_ref[i]
            inv_ref[i] = d + _dot(d, m_ref[i]).astype(BF16)
        return carry

    lax.fori_loop(0, n_levels, level, 0)

    for i, (c, h) in enumerate(items):
        r, cols = rows_of(c), cols_of(h)
        sol = _dot(inv_ref[i], jnp.concatenate([vb_ref[r, cols], kbe_ref[r, cols]], axis=1))
        ubar_ref[r, cols] = sol[:, 0:HEAD_DIM]
        w_ref[r, cols] = sol[:, HEAD_DIM:2 * HEAD_DIM].astype(BF16)

    for c in range(n_chunks):
        r = rows_of(c)
        ws = [_dot(jnp.concatenate([w_ref[r, cols_of(h)], qd_ref[r, cols_of(h)]], axis=0),
                   state_ref[h].astype(BF16)) for h in range(n_heads)]
        us = [(ubar_ref[r, cols_of(h)] - ws[h][0:c_len]).astype(BF16) for h in range(n_heads)]
        for h in range(n_heads):
            cols = cols_of(h)
            o_ref[r, cols] = ws[h][c_len:2 * c_len] + _dot(qk_ref[c * n_heads + h], us[h])
            g_last = jnp.exp(jnp.broadcast_to(gl_ref[r, 8 + h:9 + h], (c_len, HEAD_DIM))[0:1, :])
            state_ref[h] = (jnp.broadcast_to(g_last, (HEAD_DIM, HEAD_DIM)) * state_ref[h]
                            + _dot_tn(kd_ref[r, cols], us[h]))

    dn = dn_ref[...]
    for h in range(n_heads):
        cols = slice(h * HEAD_DIM, (h + 1) * HEAD_DIM)
        o = o_ref[:, cols]
        o = o * lax.rsqrt(jnp.mean(o * o, axis=-1, keepdims=True) + EPS) * dn
        out_ref[:, cols] = (o * sz_ref[:, cols].astype(F32)).astype(out_ref.dtype)


def _delta_tri_matrix():
    i = np.arange(DELTA_TILE)[:, None]
    t = np.arange(DELTA_TILE)[None, :]
    same = (i // DELTA_CHUNK) == (t // DELTA_CHUNK)
    return jnp.asarray(np.concatenate([same & (t <= i), same], axis=0).astype(np.float32), dtype=BF16)


def _delta_rule(qk, v, zg, ba, a_log, dt_bias, dn_norm):
    s = qk.shape[0]
    hps = DELTA_HEADS_PER_STEP
    width = hps * HEAD_DIM
    groups = N_HEADS_B // hps
    n_items = (DELTA_TILE // DELTA_CHUNK) * hps

    def rows(offset):
        return pl.BlockSpec((DELTA_TILE, width), lambda g, t: (t, offset + g))

    def lane_table(vals):
        tab = jnp.zeros((groups, LANES), F32).at[:, 8:8 + hps].set(vals.reshape(groups, hps).astype(F32))
        return tab.reshape(1, groups * LANES)

    bf = lambda: pltpu.VMEM((DELTA_TILE, width), BF16)
    return pl.pallas_call(
        _delta_kernel,
        out_shape=jax.ShapeDtypeStruct((s, D_DELTA), BF16),
        grid=(groups, s // DELTA_TILE),
        in_specs=[rows(0), rows(groups), rows(0), rows(0),
                  pl.BlockSpec((DELTA_TILE, LANES), lambda g, t: (t, g)),
                  pl.BlockSpec((1, LANES), lambda g, t: (0, g)),
                  pl.BlockSpec((1, LANES), lambda g, t: (0, g)),
                  pl.BlockSpec((1, HEAD_DIM), lambda g, t: (0, 0)),
                  pl.BlockSpec((2 * DELTA_TILE, DELTA_TILE), lambda g, t: (0, 0))],
        out_specs=pl.BlockSpec((DELTA_TILE, width), lambda g, t: (t, g)),
        scratch_shapes=[
            pltpu.VMEM((hps, HEAD_DIM, HEAD_DIM), F32),
            bf(), bf(), bf(), bf(), bf(),
            pltpu.VMEM((DELTA_TILE, LANES), F32),
            pltpu.VMEM((DELTA_TILE, LANES), F32),
            pltpu.VMEM((DELTA_TILE // DELTA_CHUNK, LANES, DELTA_CHUNK), F32),
            pltpu.VMEM((DELTA_TILE, width), F32),
            pltpu.VMEM((n_items, DELTA_CHUNK, DELTA_CHUNK), BF16),
            pltpu.VMEM((n_items, DELTA_CHUNK, DELTA_CHUNK), BF16),
            pltpu.VMEM((n_items, DELTA_CHUNK, DELTA_CHUNK), BF16),
            pltpu.VMEM((n_items, DELTA_CHUNK, DELTA_CHUNK), BF16),
            pltpu.VMEM((DELTA_TILE, width), F32),
            pltpu.VMEM((DELTA_TILE, width), BF16),
        ],
        compiler_params=pltpu.CompilerParams(
            dimension_semantics=("parallel", "arbitrary"), vmem_limit_bytes=VMEM_LIMIT),
        name="delta_rule",
    )(qk, qk, v, zg, ba, lane_table(a_log), lane_table(dt_bias),
      dn_norm.reshape(1, HEAD_DIM).astype(F32), _delta_tri_matrix())


def _merge_kernel(o0_ref, o1_ref, o2_ref, l0_ref, l1_ref, l2_ref, od_ref, ga_ref, gb_ref, x_ref,
                  wpa_ref, wpd_ref, wo_ref, out_ref):
    l0, l1, l2 = l0_ref[...], l1_ref[...], l2_ref[...]
    mx = jnp.maximum(jnp.maximum(l0, l1), l2)
    e0, e1, e2 = jnp.exp(l0 - mx), jnp.exp(l1 - mx), jnp.exp(l2 - mx)
    inv = 1.0 / (e0 + e1 + e2)
    w0, w1, w2 = e0 * inv, e1 * inv, e2 * inv
    parts = []
    for j in range(HEADS_PER_GROUP):
        cols = slice(j * HEAD_DIM, (j + 1) * HEAD_DIM)
        parts.append(w0[:, j:j + 1] * o0_ref[:, cols].astype(F32) + w1[:, j:j + 1] * o1_ref[:, cols].astype(F32)
                     + w2[:, j:j + 1] * o2_ref[:, cols].astype(F32))
    y_a = _dot(jnp.concatenate(parts, axis=1).astype(BF16), wpa_ref[...])
    y_b = _dot(od_ref[...], wpd_ref[...])
    merged = ga_ref[...].astype(F32) * y_a + gb_ref[...].astype(F32) * y_b
    out_ref[...] = x_ref[...] + _dot(merged.astype(BF16), wo_ref[...])


def _merge_proj(outs, lses, o_delta, zg, x, w_pa, w_pd, w_o):
    s, d = x.shape
    tm = TM_MERGE
    row = lambda width: pl.BlockSpec((tm, width), lambda i: (i, 0))
    full = lambda a: pl.BlockSpec(a.shape, lambda i: (0, 0))
    return pl.pallas_call(
        _merge_kernel,
        out_shape=jax.ShapeDtypeStruct((s, d), F32),
        grid=(s // tm,),
        in_specs=[row(D_ATTN_OUT)] * 3 + [row(LANES)] * 3 + [
            row(D_DELTA),
            pl.BlockSpec((tm, d), lambda i: (i, 1)),
            pl.BlockSpec((tm, d), lambda i: (i, 2)),
            row(d), full(w_pa), full(w_pd), full(w_o)],
        out_specs=row(d),
        compiler_params=pltpu.CompilerParams(
            dimension_semantics=("parallel",), vmem_limit_bytes=VMEM_LIMIT),
        name="merge_proj",
    )(*outs, *lses, o_delta, zg, zg, x, w_pa, w_pd, w_o)


def _ffn_kernel(h_ref, g_ref, wg_ref, wu_ref, wd_ref, gf_ref, o_ref, *, final_norm):
    h = h_ref[...]
    hn = (h * lax.rsqrt(jnp.mean(h * h, axis=-1, keepdims=True) + EPS) * g_ref[...]).astype(BF16)
    act = (_silu(_dot(hn, wg_ref[...])) * _dot(hn, wu_ref[...])).astype(BF16)
    y = h + _dot(act, wd_ref[...])
    if final_norm:
        y = y * lax.rsqrt(jnp.mean(y * y, axis=-1, keepdims=True) + EPS) * gf_ref[...]
    o_ref[...] = y


def _ffn(h, gain, w_g, w_u, w_d, gain_final, final_norm):
    s, d = h.shape
    tm = TM_FFN
    full = lambda a: pl.BlockSpec(a.shape, lambda i: (0, 0))
    vec = pl.BlockSpec((1, d), lambda i: (0, 0))
    return pl.pallas_call(
        functools.partial(_ffn_kernel, final_norm=final_norm),
        out_shape=jax.ShapeDtypeStruct((s, d), F32),
        grid=(s // tm,),
        in_specs=[pl.BlockSpec((tm, d), lambda i: (i, 0)), vec, full(w_g), full(w_u), full(w_d), vec],
        out_specs=pl.BlockSpec((tm, d), lambda i: (i, 0)),
        compiler_params=pltpu.CompilerParams(
            dimension_semantics=("parallel",), vmem_limit_bytes=VMEM_LIMIT),
        name="ffn",
    )(h, gain.reshape(1, d), w_g, w_u, w_d, gain_final.reshape(1, d))


def _gate_weight_columns(w_in, col_beta, col_a):
    hps = DELTA_HEADS_PER_STEP
    groups = N_HEADS_B // hps
    d = w_in.shape[0]
    w = jnp.zeros((d, groups, LANES), w_in.dtype)
    w = w.at[:, :, 0:hps].set(w_in[:, col_beta:col_beta + N_HEADS_B].reshape(d, groups, hps))
    w = w.at[:, :, 8:8 + hps].set(w_in[:, col_a:col_a + N_HEADS_B].reshape(d, groups, hps))
    return w.reshape(d, groups * LANES)


def kernel(x, norm_mix, w_in, conv_w, a_log, dt_bias, dn_norm, w_proj_attn, w_proj_delta,
           w_out, norm_ffn, w_gate, w_up, w_down, norm_final):
    b, s, d = x.shape
    assert b == 1 and s % (TM_PROJ * 16) == 0 and d == 8 * LANES
    depth = w_in.shape[0]
    c_attn = 3 * D_ATTN
    c_z = c_attn + 4 * D_DELTA
    c_beta, c_a = c_z, c_z + N_HEADS_B
    c_gate = c_z + 2 * N_HEADS_B
    h = x.reshape(s, d)
    for layer in range(depth):
        wl = w_in[layer]
        c_dq = c_attn
        w_qk = wl[:, c_dq:c_dq + 2 * D_DELTA].astype(BF16)
        w_v = wl[:, c_dq + 2 * D_DELTA:c_dq + 3 * D_DELTA].astype(BF16)
        w_zg = jnp.concatenate([wl[:, c_dq + 3 * D_DELTA:c_z], wl[:, c_gate:]], axis=1).astype(BF16)
        w_ba = _gate_weight_columns(wl, c_beta, c_a).astype(BF16)
        cw = conv_w[layer]
        qk_d = _proj_act(h, norm_mix[layer], w_qk, D_DELTA, conv_w=cw[:, :2 * D_DELTA], l2norm=True, q_tiles=1)
        v_d, ba = _proj_act(h, norm_mix[layer], w_v, D_DELTA, conv_w=cw[:, 2 * D_DELTA:], w_gates=w_ba)
        zg = _proj_act(h, norm_mix[layer], w_zg, D_DELTA, act="silu_first")

        outs, lses = [], []
        for g, (_, dilation) in enumerate(ATTN_PATTERNS):
            w_g = jnp.concatenate([wl[:, sec * D_ATTN + g * D_ATTN_OUT:sec * D_ATTN + (g + 1) * D_ATTN_OUT]
                                   for sec in range(3)], axis=1).astype(BF16)
            qkv_g = _norm_matmul(h, norm_mix[layer], w_g, BF16, 3 * D_ATTN_OUT, dilation)
            o_g, lse_g = _attn_group(qkv_g, g)
            outs.append(o_g)
            lses.append(lse_g)
        o_delta = _delta_rule(qk_d, v_d, zg, ba, a_log[layer], dt_bias[layer], dn_norm[layer])
        h = _merge_proj(outs, lses, o_delta, zg, h, w_proj_attn[layer].astype(BF16),
                        w_proj_delta[layer].astype(BF16), w_out[layer].astype(BF16))
        h = _ffn(h, norm_ffn[layer], w_gate[layer].astype(BF16), w_up[layer].astype(BF16),
                 w_down[layer].astype(BF16), norm_final, layer == depth - 1)
    return h.reshape(b, s, d)
```

```python
import functools

import numpy as np
import jax
import jax.numpy as jnp
from jax import lax
from jax.experimental import pallas as pl
from jax.experimental.pallas import tpu as pltpu

F32 = jnp.float32
BF16 = jnp.bfloat16

EPS = 1e-6
LANES = 128
HEAD_DIM = 128
ATTN_PATTERNS = ((128, 1), (512, 4), (2048, 16))
HEADS_PER_GROUP = 4
N_HEADS_A = len(ATTN_PATTERNS) * HEADS_PER_GROUP
D_ATTN = N_HEADS_A * HEAD_DIM
D_ATTN_OUT = HEADS_PER_GROUP * HEAD_DIM
ATTN_WINDOW_SUB = 128
N_HEADS_B = 8
D_DELTA = N_HEADS_B * HEAD_DIM
CONV_WIDTH = 4

VMEM_LIMIT = 56 * 1024 * 1024

TM_PROJ = 1024
CONV_HALO = 16
PROJ_CHUNK = 256
PROJ_STEPS = 6
PERM_ROWS = 256
TQ_ATTN = 512
DELTA_CHUNK = 128
DELTA_TILE = 512
DELTA_HEADS_PER_STEP = 4
TM_MERGE = 512
TM_FFN = 512


def _sigmoid(v):
    return 1.0 / (1.0 + jnp.exp(-v))


def _silu(v):
    return v * _sigmoid(v)


def _dot(a, b):
    return jnp.dot(a, b, preferred_element_type=F32)


def _dot_nt(a, b):
    return lax.dot_general(a, b, (((1,), (1,)), ((), ())), preferred_element_type=F32)


def _dot_tn(a, b):
    return lax.dot_general(a, b, (((0,), (0,)), ((), ())), preferred_element_type=F32)


def _stride_permutation(dilation):
    piece = PERM_ROWS // dilation
    i = np.arange(PERM_ROWS)
    src = (i % piece) * dilation + i // piece
    return jnp.asarray((src[:, None] == np.arange(PERM_ROWS)[None, :]).astype(np.float32), dtype=BF16)


def _in_proj_kernel(x_ref, xh_ref, g_ref, p1_ref, p2_ref, wa0_ref, wa1_ref, wa2_ref, wc_ref, wz_ref, wba_ref,
                    cw_ref, a0_ref, a1_ref, a2_ref, c_ref, z_ref, ba_ref, xn_ref, xn1_ref, xn2_ref):
    i, j = pl.program_id(0), pl.program_id(1)
    tm = TM_PROJ
    chunk = PROJ_CHUNK

    def norm(x):
        return (x * lax.rsqrt(jnp.mean(x * x, axis=-1, keepdims=True) + EPS) * g_ref[...]).astype(BF16)

    @pl.when(j == 0)
    def _():
        xn = norm(x_ref[...])
        xn_ref[0:tm, :] = xn
        xn_ref[tm:tm + CONV_HALO, :] = jnp.where(i == 0, jnp.zeros((), BF16), norm(xh_ref[...]))
        ba_ref[...] = _dot(xn, wba_ref[...])
        for perm_ref, dst_ref, (_, dilation) in ((p1_ref, xn1_ref, ATTN_PATTERNS[1]),
                                                 (p2_ref, xn2_ref, ATTN_PATTERNS[2])):
            rows = tm // dilation
            piece = PERM_ROWS // dilation
            for g in range(tm // PERM_ROWS):
                y = _dot(perm_ref[...], xn[g * PERM_ROWS:(g + 1) * PERM_ROWS, :]).astype(BF16)
                for sub in range(dilation):
                    dst_ref[sub * rows + g * piece:sub * rows + (g + 1) * piece, :] = (
                        y[sub * piece:(sub + 1) * piece, :])

    def attn_store(acc, o_ref):
        o_ref[...] = acc.astype(BF16).reshape(o_ref.shape)

    def conv_store(acc, c0):
        x = acc[0:tm]
        halo = acc[tm + CONV_HALO - 8:tm + CONV_HALO]
        w = cw_ref[:, c0:c0 + chunk]
        row8 = lax.broadcasted_iota(jnp.int32, halo.shape, 0)
        y = w[CONV_WIDTH - 1:CONV_WIDTH, :] * x
        for s in range(1, CONV_WIDTH):
            xs = pltpu.roll(x, s, 0)
            head = jnp.where(row8 < s, pltpu.roll(halo, s, 0), xs[0:8])
            xs = jnp.concatenate([head, xs[8:]], axis=0)
            y = y + w[CONV_WIDTH - 1 - s:CONV_WIDTH - s, :] * xs
        y = y * _sigmoid(y)
        q_scale = jnp.where(j < PROJ_STEPS // 3, HEAD_DIM ** -0.5, 1.0)
        for h in range(chunk // HEAD_DIM):
            seg = y[:, h * HEAD_DIM:(h + 1) * HEAD_DIM]
            inv = lax.rsqrt(jnp.sum(seg * seg, axis=-1, keepdims=True) + EPS) * q_scale
            factor = jnp.where(j < 2 * PROJ_STEPS // 3, inv, 1.0)
            c_ref[:, c0 + h * HEAD_DIM:c0 + (h + 1) * HEAD_DIM] = (seg * factor).astype(BF16)

    def gate_store(acc, c0):
        sig = _sigmoid(acc)
        z_ref[:, c0:c0 + chunk] = (sig * jnp.where(j < PROJ_STEPS // 3, acc, 1.0)).astype(BF16)

    acc_a0 = _dot(xn_ref[0:tm, :], wa0_ref[...])
    acc_c0 = _dot(xn_ref[...], wc_ref[:, 0:chunk])
    attn_store(acc_a0, a0_ref)
    acc_a1 = _dot(xn1_ref[...], wa1_ref[...])
    conv_store(acc_c0, 0)
    acc_z0 = _dot(xn_ref[0:tm, :], wz_ref[:, 0:chunk])
    attn_store(acc_a1, a1_ref)
    acc_c1 = _dot(xn_ref[...], wc_ref[:, chunk:2 * chunk])
    gate_store(acc_z0, 0)
    acc_a2 = _dot(xn2_ref[...], wa2_ref[...])
    conv_store(acc_c1, chunk)
    acc_z1 = _dot(xn_ref[0:tm, :], wz_ref[:, chunk:2 * chunk])
    attn_store(acc_a2, a2_ref)
    gate_store(acc_z1, chunk)


def _in_proj(x, gain, w_attn, w_conv, w_zg, w_ba, conv_w):
    s, d = x.shape
    dil = [p[1] for p in ATTN_PATTERNS]
    assert dil[0] == 1 and w_conv.shape[1] == 2 * PROJ_CHUNK * PROJ_STEPS == w_zg.shape[1]
    assert all(w.shape[1] == PROJ_CHUNK * PROJ_STEPS for w in w_attn)
    halo_blocks = TM_PROJ // CONV_HALO
    const = lambda a: pl.BlockSpec(a.shape, lambda i, j: (0,) * a.ndim)
    cols = lambda width: pl.BlockSpec((d, width), lambda i, j: (0, j))
    perms = [_stride_permutation(dl) for dl in dil[1:]]
    gain2 = gain.reshape(1, d)
    ng = w_ba.shape[1]
    return pl.pallas_call(
        _in_proj_kernel,
        out_shape=[jax.ShapeDtypeStruct((dl, s // dl, PROJ_CHUNK * PROJ_STEPS), BF16) for dl in dil] + [
            jax.ShapeDtypeStruct((s, w_conv.shape[1]), BF16),
            jax.ShapeDtypeStruct((s, w_zg.shape[1]), BF16),
            jax.ShapeDtypeStruct((s, ng), F32)],
        grid=(s // TM_PROJ, PROJ_STEPS),
        in_specs=[pl.BlockSpec((TM_PROJ, d), lambda i, j: (i, 0)),
                  pl.BlockSpec((CONV_HALO, d), lambda i, j: (jnp.maximum(i * halo_blocks - 1, 0), 0)),
                  const(gain2), const(perms[0]), const(perms[1]),
                  cols(PROJ_CHUNK), cols(PROJ_CHUNK), cols(PROJ_CHUNK),
                  cols(2 * PROJ_CHUNK), cols(2 * PROJ_CHUNK), const(w_ba),
                  pl.BlockSpec((CONV_WIDTH, 2 * PROJ_CHUNK), lambda i, j: (0, j))],
        out_specs=[pl.BlockSpec((dl, TM_PROJ // dl, PROJ_CHUNK), lambda i, j: (0, i, j)) for dl in dil] + [
            pl.BlockSpec((TM_PROJ, 2 * PROJ_CHUNK), lambda i, j: (i, j)),
            pl.BlockSpec((TM_PROJ, 2 * PROJ_CHUNK), lambda i, j: (i, j)),
            pl.BlockSpec((TM_PROJ, ng), lambda i, j: (i, 0))],
        scratch_shapes=[pltpu.VMEM((TM_PROJ + CONV_HALO, d), BF16),
                        pltpu.VMEM((TM_PROJ, d), BF16),
                        pltpu.VMEM((TM_PROJ, d), BF16)],
        compiler_params=pltpu.CompilerParams(
            dimension_semantics=("parallel", "arbitrary"), vmem_limit_bytes=VMEM_LIMIT),
        name="in_proj",
    )(x, x, gain2, perms[0], perms[1], *w_attn, w_conv, w_zg, w_ba, conv_w)


def _attn_bias_table(group):
    _, dilation = ATTN_PATTERNS[group]
    heads = np.arange(group * HEADS_PER_GROUP, (group + 1) * HEADS_PER_GROUP, dtype=np.float32)
    slopes = np.exp2(np.float32(-8.0) * (heads + 1) / np.float32(N_HEADS_A)).astype(np.float32)
    i = np.arange(LANES)[:, None]
    c = np.arange(2 * LANES)[None, :]
    delta = LANES + i - c
    valid = (delta >= 0) & (delta <= ATTN_WINDOW_SUB)
    bias = -(slopes * np.float32(dilation))[:, None, None] * delta.astype(np.float32)[None]
    return jnp.asarray(np.where(valid[None], bias, -np.inf).astype(np.float32))


def _attn_kernel(q_ref, kc_ref, kp_ref, vc_ref, vp_ref, bias_ref, o_ref, lse_ref):
    first_tile = pl.program_id(1) == 0
    scale = HEAD_DIM ** -0.5
    lane = lax.broadcasted_iota(jnp.int32, (LANES, LANES), 1)
    key_col = lax.broadcasted_iota(jnp.int32, (LANES, 2 * LANES), 1)
    for b in range(TQ_ATTN // LANES):
        rows = slice(b * LANES, (b + 1) * LANES)
        lse_tile = jnp.zeros((LANES, LANES), F32)
        for j in range(HEADS_PER_GROUP):
            cols = slice(j * HEAD_DIM, (j + 1) * HEAD_DIM)
            q = q_ref[rows, cols]
            if b == 0:
                k2 = jnp.concatenate([kp_ref[:, cols], kc_ref[0:LANES, cols]], axis=0)
                v2 = jnp.concatenate([vp_ref[:, cols], vc_ref[0:LANES, cols]], axis=0)
            else:
                k2 = kc_ref[(b - 1) * LANES:(b + 1) * LANES, cols]
                v2 = vc_ref[(b - 1) * LANES:(b + 1) * LANES, cols]
            s = _dot_nt(q, k2) * scale + bias_ref[j]
            if b == 0:
                s = jnp.where(jnp.logical_and(first_tile, key_col < LANES), -jnp.inf, s)
            m = jnp.max(s, axis=-1, keepdims=True)
            p = jnp.exp(s - m)
            den = jnp.sum(p, axis=-1, keepdims=True)
            o_ref[rows, cols] = (_dot(p.astype(BF16), v2) / den).astype(o_ref.dtype)
            lse_tile = jnp.where(lane == j, m + jnp.log(den), lse_tile)
        lse_ref[rows, :] = lse_tile


def _attn_group(qkv, group):
    d, r, _ = qkv.shape
    halo_step = TQ_ATTN // LANES

    def cur(section):
        return pl.BlockSpec((None, TQ_ATTN, D_ATTN_OUT), lambda sub, n: (sub, n, section))

    def prev(section):
        return pl.BlockSpec((None, LANES, D_ATTN_OUT),
                            lambda sub, n: (sub, jnp.maximum(n * halo_step - 1, 0), section))

    return pl.pallas_call(
        _attn_kernel,
        out_shape=(jax.ShapeDtypeStruct((d, r, D_ATTN_OUT), BF16),
                   jax.ShapeDtypeStruct((d, r, LANES), F32)),
        grid=(d, r // TQ_ATTN),
        in_specs=[cur(0), cur(1), prev(1), cur(2), prev(2),
                  pl.BlockSpec((HEADS_PER_GROUP, LANES, 2 * LANES), lambda sub, n: (0, 0, 0))],
        out_specs=(pl.BlockSpec((None, TQ_ATTN, D_ATTN_OUT), lambda sub, n: (sub, n, 0)),
                   pl.BlockSpec((None, TQ_ATTN, LANES), lambda sub, n: (sub, n, 0))),
        compiler_params=pltpu.CompilerParams(
            dimension_semantics=("parallel", "arbitrary"), vmem_limit_bytes=VMEM_LIMIT),
        name=f"attn_group{group}",
    )(qkv, qkv, qkv, qkv, qkv, _attn_bias_table(group))


def _delta_kernel(q_ref, k_ref, v_ref, sz_ref, ba_ref, alog_ref, dtb_ref, dn_ref, tri_ref, out_ref,
                  state_ref, kbeta_ref, qd_ref, kd_ref, vb_ref, kbe_ref,
                  gc_ref, gl_ref, gct_ref, o_ref, b_ref, inv_ref, m_ref, qk_ref, ubar_ref, w_ref):
    c_len = DELTA_CHUNK
    n_heads = DELTA_HEADS_PER_STEP
    tile = pl.program_id(1)

    @pl.when(tile == 0)
    def _():
        state_ref[...] = jnp.zeros_like(state_ref)

    ba = ba_ref[...]
    lane = lax.broadcasted_iota(jnp.int32, ba.shape, 1)
    pre = ba + dtb_ref[...]
    softplus = jnp.maximum(pre, 0.0) + jnp.log(1.0 + jnp.exp(-jnp.abs(pre)))
    bg = jnp.where(lane < 8, _sigmoid(ba), -jnp.exp(alog_ref[...]) * softplus)
    hi = bg.astype(BF16)
    r1 = bg - hi.astype(F32)
    mid = r1.astype(BF16)
    lo = (r1 - mid.astype(F32)).astype(BF16)
    sums = _dot(tri_ref[...], jnp.concatenate([hi, mid, lo], axis=1))
    sums = sums[:, 0:LANES] + sums[:, LANES:2 * LANES] + sums[:, 2 * LANES:3 * LANES]
    gc = sums[0:DELTA_TILE]
    gl = sums[DELTA_TILE:2 * DELTA_TILE]
    gc_ref[...] = gc
    gl_ref[...] = gl
    for c in range(DELTA_TILE // c_len):
        gct_ref[c] = jnp.transpose(gc[c * c_len:(c + 1) * c_len, :])

    for h in range(n_heads):
        cols = slice(h * HEAD_DIM, (h + 1) * HEAD_DIM)
        beta = bg[:, h:h + 1]
        gch = gc[:, 8 + h:9 + h]
        e_gc = jnp.exp(gch)
        kn = k_ref[:, cols].astype(F32)
        qd_ref[:, cols] = (q_ref[:, cols].astype(F32) * e_gc).astype(BF16)
        kbeta_ref[:, cols] = (kn * beta).astype(BF16)
        kbe_ref[:, cols] = (kn * (beta * e_gc)).astype(BF16)
        kd_ref[:, cols] = (kn * jnp.exp(gl[:, 8 + h:9 + h] - gch)).astype(BF16)
        vb_ref[:, cols] = (v_ref[:, cols].astype(F32) * beta).astype(BF16)

    row = lax.broadcasted_iota(jnp.int32, (c_len, c_len), 0)
    col = lax.broadcasted_iota(jnp.int32, (c_len, c_len), 1)
    lower = row >= col
    strict = row > col
    eye = jnp.where(row == col, 1.0, 0.0).astype(F32)
    pair = jnp.right_shift(row, 1) == jnp.right_shift(col, 1)
    n_levels = c_len.bit_length() - 2

    n_chunks = DELTA_TILE // c_len
    items = [(c, h) for c in range(n_chunks) for h in range(n_heads)]

    def rows_of(c):
        return slice(c * c_len, (c + 1) * c_len)

    def cols_of(h):
        return slice(h * HEAD_DIM, (h + 1) * HEAD_DIM)

    for i, (c, h) in enumerate(items):
        r, cols = rows_of(c), cols_of(h)
        kq = _dot_nt(jnp.concatenate([kbeta_ref[r, cols], q_ref[r, cols]], axis=0), k_ref[r, cols])
        g_col = gc_ref[r, 8 + h:9 + h]
        g_row = gct_ref[c, 8 + h:9 + h, :]
        decay = jnp.exp(jnp.where(lower, g_col - g_row, -jnp.inf))
        p = jnp.where(strict, -(kq[0:c_len] * decay), 0.0)
        b_ref[i] = p.astype(BF16)
        inv_ref[i] = (eye + jnp.where(pair, p, 0.0)).astype(BF16)
        qk_ref[i] = (kq[c_len:2 * c_len] * decay).astype(BF16)

    def level(l, carry):
        lb = l + 1
        off = jnp.logical_and(jnp.right_shift(row, lb) == jnp.right_shift(col, lb) + 1,
                              jnp.bitwise_and(jnp.right_shift(col, lb), 1) == 0)
        for i in range(len(items)):
            m_ref[i] = jnp.where(off, _dot(b_ref[i], inv_ref[i]), 0.0).astype(BF16)
        for i in range(len(items)):
            d = inv_ref[i]
            inv_ref[i] = d + _dot(d, m_ref[i]).astype(BF16)
        return carry

    lax.fori_loop(0, n_levels, level, 0)

    for i, (c, h) in enumerate(items):
        r, cols = rows_of(c), cols_of(h)
        sol = _dot(inv_ref[i], jnp.concatenate([vb_ref[r, cols], kbe_ref[r, cols]], axis=1))
        ubar_ref[r, cols] = sol[:, 0:HEAD_DIM]
        w_ref[r, cols] = sol[:, HEAD_DIM:2 * HEAD_DIM].astype(BF16)

    for c in range(n_chunks):
        r = rows_of(c)
        ws = [_dot(jnp.concatenate([w_ref[r, cols_of(h)], qd_ref[r, cols_of(h)]], axis=0),
                   state_ref[h].astype(BF16)) for h in range(n_heads)]
        us = [(ubar_ref[r, cols_of(h)] - ws[h][0:c_len]).astype(BF16) for h in range(n_heads)]
        for h in range(n_heads):
            cols = cols_of(h)
            o_ref[r, cols] = ws[h][c_len:2 * c_len] + _dot(qk_ref[c * n_heads + h], us[h])
            g_last = jnp.exp(jnp.broadcast_to(gl_ref[r, 8 + h:9 + h], (c_len, HEAD_DIM))[0:1, :])
            state_ref[h] = (jnp.broadcast_to(g_last, (HEAD_DIM, HEAD_DIM)) * state_ref[h]
                            + _dot_tn(kd_ref[r, cols], us[h]))

    dn = dn_ref[...]
    for h in range(n_heads):
        cols = slice(h * HEAD_DIM, (h + 1) * HEAD_DIM)
        o = o_ref[:, cols]
        o = o * lax.rsqrt(jnp.mean(o * o, axis=-1, keepdims=True) + EPS) * dn
        out_ref[:, cols] = (o * sz_ref[:, cols].astype(F32)).astype(out_ref.dtype)


def _delta_tri_matrix():
    i = np.arange(DELTA_TILE)[:, None]
    t = np.arange(DELTA_TILE)[None, :]
    same = (i // DELTA_CHUNK) == (t // DELTA_CHUNK)
    return jnp.asarray(np.concatenate([same & (t <= i), same], axis=0).astype(np.float32), dtype=BF16)


def _delta_rule(qkv, zg, ba, a_log, dt_bias, dn_norm):
    s = qkv.shape[0]
    hps = DELTA_HEADS_PER_STEP
    width = hps * HEAD_DIM
    groups = N_HEADS_B // hps
    n_items = (DELTA_TILE // DELTA_CHUNK) * hps

    def rows(offset):
        return pl.BlockSpec((DELTA_TILE, width), lambda g, t: (t, offset + g))

    def lane_table(vals):
        tab = jnp.zeros((groups, LANES), F32).at[:, 8:8 + hps].set(vals.reshape(groups, hps).astype(F32))
        return tab.reshape(1, groups * LANES)

    bf = lambda: pltpu.VMEM((DELTA_TILE, width), BF16)
    return pl.pallas_call(
        _delta_kernel,
        out_shape=jax.ShapeDtypeStruct((s, D_DELTA), BF16),
        grid=(groups, s // DELTA_TILE),
        in_specs=[rows(0), rows(groups), rows(2 * groups), rows(0),
                  pl.BlockSpec((DELTA_TILE, LANES), lambda g, t: (t, g)),
                  pl.BlockSpec((1, LANES), lambda g, t: (0, g)),
                  pl.BlockSpec((1, LANES), lambda g, t: (0, g)),
                  pl.BlockSpec((1, HEAD_DIM), lambda g, t: (0, 0)),
                  pl.BlockSpec((2 * DELTA_TILE, DELTA_TILE), lambda g, t: (0, 0))],
        out_specs=pl.BlockSpec((DELTA_TILE, width), lambda g, t: (t, g)),
        scratch_shapes=[
            pltpu.VMEM((hps, HEAD_DIM, HEAD_DIM), F32),
            bf(), bf(), bf(), bf(), bf(),
            pltpu.VMEM((DELTA_TILE, LANES), F32),
            pltpu.VMEM((DELTA_TILE, LANES), F32),
            pltpu.VMEM((DELTA_TILE // DELTA_CHUNK, LANES, DELTA_CHUNK), F32),
            pltpu.VMEM((DELTA_TILE, width), F32),
            pltpu.VMEM((n_items, DELTA_CHUNK, DELTA_CHUNK), BF16),
            pltpu.VMEM((n_items, DELTA_CHUNK, DELTA_CHUNK), BF16),
            pltpu.VMEM((n_items, DELTA_CHUNK, DELTA_CHUNK), BF16),
            pltpu.VMEM((n_items, DELTA_CHUNK, DELTA_CHUNK), BF16),
            pltpu.VMEM((DELTA_TILE, width), F32),
            pltpu.VMEM((DELTA_TILE, width), BF16),
        ],
        compiler_params=pltpu.CompilerParams(
            dimension_semantics=("parallel", "arbitrary"), vmem_limit_bytes=VMEM_LIMIT),
        name="delta_rule",
    )(qkv, qkv, qkv, zg, ba, lane_table(a_log), lane_table(dt_bias),
      dn_norm.reshape(1, HEAD_DIM).astype(F32), _delta_tri_matrix())


def _merge_kernel(o0_ref, o1_ref, o2_ref, l0_ref, l1_ref, l2_ref, u1_ref, u2_ref, od_ref, ga_ref, gb_ref, x_ref,
                  wpa_ref, wpd_ref, wo_ref, out_ref):
    tm = out_ref.shape[0]

    def to_positions(o_ref, l_ref, u_ref):
        o = _dot(u_ref[...], o_ref[...].reshape(tm, D_ATTN_OUT))
        lse = l_ref[...].reshape(tm, LANES)
        hi = lse.astype(BF16)
        r1 = lse - hi.astype(F32)
        mid = r1.astype(BF16)
        lo = (r1 - mid.astype(F32)).astype(BF16)
        parts = _dot(u_ref[...], jnp.concatenate([hi, mid, lo], axis=1))
        return o, parts[:, 0:LANES] + parts[:, LANES:2 * LANES] + parts[:, 2 * LANES:3 * LANES]

    o0 = o0_ref[...].reshape(tm, D_ATTN_OUT).astype(F32)
    l0 = l0_ref[...].reshape(tm, LANES)
    o1, l1 = to_positions(o1_ref, l1_ref, u1_ref)
    o2, l2 = to_positions(o2_ref, l2_ref, u2_ref)
    mx = jnp.maximum(jnp.maximum(l0, l1), l2)
    e0, e1, e2 = jnp.exp(l0 - mx), jnp.exp(l1 - mx), jnp.exp(l2 - mx)
    inv = 1.0 / (e0 + e1 + e2)
    w0, w1, w2 = e0 * inv, e1 * inv, e2 * inv
    parts = []
    for j in range(HEADS_PER_GROUP):
        cols = slice(j * HEAD_DIM, (j + 1) * HEAD_DIM)
        parts.append(w0[:, j:j + 1] * o0[:, cols] + w1[:, j:j + 1] * o1[:, cols] + w2[:, j:j + 1] * o2[:, cols])
    y_a = _dot(jnp.concatenate(parts, axis=1).astype(BF16), wpa_ref[...])
    y_b = _dot(od_ref[...], wpd_ref[...])
    merged = ga_ref[...].astype(F32) * y_a + gb_ref[...].astype(F32) * y_b
    out_ref[...] = x_ref[...] + _dot(merged.astype(BF16), wo_ref[...])


def _position_permutation(dilation, tm):
    p = np.arange(tm)
    src = (p % dilation) * (tm // dilation) + p // dilation
    return jnp.asarray((src[:, None] == np.arange(tm)[None, :]).astype(np.float32), dtype=BF16)


def _merge_proj(outs, lses, o_delta, zg, x, w_pa, w_pd, w_o):
    s, d = x.shape
    tm = TM_MERGE
    row = lambda width: pl.BlockSpec((tm, width), lambda i: (i, 0))
    full = lambda a: pl.BlockSpec(a.shape, lambda i: (0, 0))
    grouped = lambda a: pl.BlockSpec((a.shape[0], tm // a.shape[0], a.shape[2]), lambda i: (0, i, 0))
    unperm = [_position_permutation(o.shape[0], tm) for o in outs[1:]]
    return pl.pallas_call(
        _merge_kernel,
        out_shape=jax.ShapeDtypeStruct((s, d), F32),
        grid=(s // tm,),
        in_specs=[grouped(o) for o in outs] + [grouped(l) for l in lses] + [full(u) for u in unperm] + [
            row(D_DELTA),
            pl.BlockSpec((tm, d), lambda i: (i, 1)),
            pl.BlockSpec((tm, d), lambda i: (i, 2)),
            row(d), full(w_pa), full(w_pd), full(w_o)],
        out_specs=row(d),
        compiler_params=pltpu.CompilerParams(
            dimension_semantics=("parallel",), vmem_limit_bytes=VMEM_LIMIT),
        name="merge_proj",
    )(*outs, *lses, *unperm, o_delta, zg, zg, x, w_pa, w_pd, w_o)


def _ffn_kernel(h_ref, g_ref, wg_ref, wu_ref, wd_ref, gf_ref, o_ref, *, final_norm):
    h = h_ref[...]
    hn = (h * lax.rsqrt(jnp.mean(h * h, axis=-1, keepdims=True) + EPS) * g_ref[...]).astype(BF16)
    act = (_silu(_dot(hn, wg_ref[...])) * _dot(hn, wu_ref[...])).astype(BF16)
    y = h + _dot(act, wd_ref[...])
    if final_norm:
        y = y * lax.rsqrt(jnp.mean(y * y, axis=-1, keepdims=True) + EPS) * gf_ref[...]
    o_ref[...] = y


def _ffn(h, gain, w_g, w_u, w_d, gain_final, final_norm):
    s, d = h.shape
    tm = TM_FFN
    once = lambda a: pl.BlockSpec(a.shape, lambda i: (0, 0), pipeline_mode=pl.Buffered(1))
    vec = pl.BlockSpec((1, d), lambda i: (0, 0))
    return pl.pallas_call(
        functools.partial(_ffn_kernel, final_norm=final_norm),
        out_shape=jax.ShapeDtypeStruct((s, d), F32),
        grid=(s // tm,),
        in_specs=[pl.BlockSpec((tm, d), lambda i: (i, 0)), vec, once(w_g), once(w_u), once(w_d), vec],
        out_specs=pl.BlockSpec((tm, d), lambda i: (i, 0)),
        compiler_params=pltpu.CompilerParams(
            dimension_semantics=("parallel",), vmem_limit_bytes=VMEM_LIMIT),
        name="ffn",
    )(h, gain.reshape(1, d), w_g, w_u, w_d, gain_final.reshape(1, d))


def _gate_weight_columns(w_in, col_beta, col_a):
    hps = DELTA_HEADS_PER_STEP
    groups = N_HEADS_B // hps
    d = w_in.shape[0]
    w = jnp.zeros((d, groups, LANES), w_in.dtype)
    w = w.at[:, :, 0:hps].set(w_in[:, col_beta:col_beta + N_HEADS_B].reshape(d, groups, hps))
    w = w.at[:, :, 8:8 + hps].set(w_in[:, col_a:col_a + N_HEADS_B].reshape(d, groups, hps))
    return w.reshape(d, groups * LANES)


def kernel(x, norm_mix, w_in, conv_w, a_log, dt_bias, dn_norm, w_proj_attn, w_proj_delta,
           w_out, norm_ffn, w_gate, w_up, w_down, norm_final):
    b, s, d = x.shape
    assert b == 1 and s % (TM_PROJ * 16) == 0 and d == 8 * LANES
    depth = w_in.shape[0]
    c_attn = 3 * D_ATTN
    c_z = c_attn + 4 * D_DELTA
    c_beta, c_a = c_z, c_z + N_HEADS_B
    c_gate = c_z + 2 * N_HEADS_B
    h = x.reshape(s, d)
    for layer in range(depth):
        wl = w_in[layer]
        c_dq = c_attn
        w_attn = [jnp.concatenate([wl[:, sec * D_ATTN + g * D_ATTN_OUT:sec * D_ATTN + (g + 1) * D_ATTN_OUT]
                                   for sec in range(3)], axis=1).astype(BF16) for g in range(len(ATTN_PATTERNS))]
        w_conv = wl[:, c_dq:c_dq + 3 * D_DELTA].astype(BF16)
        w_zg = jnp.concatenate([wl[:, c_dq + 3 * D_DELTA:c_z], wl[:, c_gate:]], axis=1).astype(BF16)
        w_ba = _gate_weight_columns(wl, c_beta, c_a).astype(BF16)
        *qkv_attn, qkv_d, zg, ba = _in_proj(h, norm_mix[layer], w_attn, w_conv, w_zg, w_ba, conv_w[layer])

        outs, lses = zip(*[_attn_group(qkv_g, g) for g, qkv_g in enumerate(qkv_attn)])
        o_delta = _delta_rule(qkv_d, zg, ba, a_log[layer], dt_bias[layer], dn_norm[layer])
        h = _merge_proj(outs, lses, o_delta, zg, h, w_proj_attn[layer].astype(BF16),
                        w_proj_delta[layer].astype(BF16), w_out[layer].astype(BF16))
        h = _ffn(h, norm_ffn[layer], w_gate[layer].astype(BF16), w_up[layer].astype(BF16),
                 w_down[layer].astype(BF16), norm_final, layer == depth - 1)
    return h.reshape(b, s, d)
```

```python
import functools

import numpy as np
import jax
import jax.numpy as jnp
from jax import lax
from jax.experimental import pallas as pl
from jax.experimental.pallas import tpu as pltpu

F32 = jnp.float32
BF16 = jnp.bfloat16

EPS = 1e-6
LANES = 128
HEAD_DIM = 128
ATTN_PATTERNS = ((128, 1), (512, 4), (2048, 16))
HEADS_PER_GROUP = 4
N_HEADS_A = len(ATTN_PATTERNS) * HEADS_PER_GROUP
D_ATTN = N_HEADS_A * HEAD_DIM
D_ATTN_OUT = HEADS_PER_GROUP * HEAD_DIM
ATTN_WINDOW_SUB = 128
N_HEADS_B = 8
D_DELTA = N_HEADS_B * HEAD_DIM
CONV_WIDTH = 4

VMEM_LIMIT = 56 * 1024 * 1024

TM_PROJ = 1024
CONV_HALO = 16
PROJ_CHUNK = 256
PROJ_STEPS = 6
PERM_ROWS = 256
TQ_ATTN = 512
DELTA_CHUNK = 128
DELTA_TILE = 512
DELTA_HEADS_PER_STEP = 8
TM_MERGE = 512
TM_FFN = 512


def _sigmoid(v):
    return 1.0 / (1.0 + jnp.exp(-v))


def _silu(v):
    return v * _sigmoid(v)


def _dot(a, b):
    return jnp.dot(a, b, preferred_element_type=F32)


def _dot_nt(a, b):
    return lax.dot_general(a, b, (((1,), (1,)), ((), ())), preferred_element_type=F32)


def _dot_tn(a, b):
    return lax.dot_general(a, b, (((0,), (0,)), ((), ())), preferred_element_type=F32)


def _stride_permutation(dilation):
    piece = PERM_ROWS // dilation
    i = np.arange(PERM_ROWS)
    src = (i % piece) * dilation + i // piece
    return jnp.asarray((src[:, None] == np.arange(PERM_ROWS)[None, :]).astype(np.float32), dtype=BF16)


def _in_proj_kernel(x_ref, xh_ref, g_ref, p1_ref, p2_ref, wa0_ref, wa1_ref, wa2_ref, wc_ref, wz_ref, wba_ref,
                    cw_ref, a0_ref, a1_ref, a2_ref, c_ref, z_ref, ba_ref, xn_ref, xn1_ref, xn2_ref):
    i, j = pl.program_id(0), pl.program_id(1)
    tm = TM_PROJ
    chunk = PROJ_CHUNK

    def norm(x):
        return (x * lax.rsqrt(jnp.mean(x * x, axis=-1, keepdims=True) + EPS) * g_ref[...]).astype(BF16)

    @pl.when(j == 0)
    def _():
        xn = norm(x_ref[...])
        xn_ref[0:tm, :] = xn
        xn_ref[tm:tm + CONV_HALO, :] = jnp.where(i == 0, jnp.zeros((), BF16), norm(xh_ref[...]))
        ba_ref[...] = _dot(xn, wba_ref[...])
        for perm_ref, dst_ref, (_, dilation) in ((p1_ref, xn1_ref, ATTN_PATTERNS[1]),
                                                 (p2_ref, xn2_ref, ATTN_PATTERNS[2])):
            rows = tm // dilation
            piece = PERM_ROWS // dilation
            for g in range(tm // PERM_ROWS):
                y = _dot(perm_ref[...], xn[g * PERM_ROWS:(g + 1) * PERM_ROWS, :]).astype(BF16)
                for sub in range(dilation):
                    dst_ref[sub * rows + g * piece:sub * rows + (g + 1) * piece, :] = (
                        y[sub * piece:(sub + 1) * piece, :])

    def attn_store(acc, o_ref):
        o_ref[...] = acc.astype(BF16).reshape(o_ref.shape)

    def conv_store(acc, c0):
        x = acc[0:tm]
        halo = acc[tm + CONV_HALO - 8:tm + CONV_HALO]
        w = cw_ref[:, c0:c0 + chunk]
        row8 = lax.broadcasted_iota(jnp.int32, halo.shape, 0)
        y = w[CONV_WIDTH - 1:CONV_WIDTH, :] * x
        for s in range(1, CONV_WIDTH):
            xs = pltpu.roll(x, s, 0)
            head = jnp.where(row8 < s, pltpu.roll(halo, s, 0), xs[0:8])
            xs = jnp.concatenate([head, xs[8:]], axis=0)
            y = y + w[CONV_WIDTH - 1 - s:CONV_WIDTH - s, :] * xs
        y = y * _sigmoid(y)
        q_scale = jnp.where(j < PROJ_STEPS // 3, HEAD_DIM ** -0.5, 1.0)
        for h in range(chunk // HEAD_DIM):
            seg = y[:, h * HEAD_DIM:(h + 1) * HEAD_DIM]
            inv = lax.rsqrt(jnp.sum(seg * seg, axis=-1, keepdims=True) + EPS) * q_scale
            factor = jnp.where(j < 2 * PROJ_STEPS // 3, inv, 1.0)
            c_ref[:, c0 + h * HEAD_DIM:c0 + (h + 1) * HEAD_DIM] = (seg * factor).astype(BF16)

    def gate_store(acc, c0):
        sig = _sigmoid(acc)
        z_ref[:, c0:c0 + chunk] = (sig * jnp.where(j < PROJ_STEPS // 3, acc, 1.0)).astype(BF16)

    acc_a0 = _dot(xn_ref[0:tm, :], wa0_ref[...])
    acc_c0 = _dot(xn_ref[...], wc_ref[:, 0:chunk])
    attn_store(acc_a0, a0_ref)
    acc_a1 = _dot(xn1_ref[...], wa1_ref[...])
    conv_store(acc_c0, 0)
    acc_z0 = _dot(xn_ref[0:tm, :], wz_ref[:, 0:chunk])
    attn_store(acc_a1, a1_ref)
    acc_c1 = _dot(xn_ref[...], wc_ref[:, chunk:2 * chunk])
    gate_store(acc_z0, 0)
    acc_a2 = _dot(xn2_ref[...], wa2_ref[...])
    conv_store(acc_c1, chunk)
    acc_z1 = _dot(xn_ref[0:tm, :], wz_ref[:, chunk:2 * chunk])
    attn_store(acc_a2, a2_ref)
    gate_store(acc_z1, chunk)


def _in_proj(x, gain, w_attn, w_conv, w_zg, w_ba, conv_w):
    s, d = x.shape
    dil = [p[1] for p in ATTN_PATTERNS]
    assert dil[0] == 1 and w_conv.shape[1] == 2 * PROJ_CHUNK * PROJ_STEPS == w_zg.shape[1]
    assert all(w.shape[1] == PROJ_CHUNK * PROJ_STEPS for w in w_attn)
    halo_blocks = TM_PROJ // CONV_HALO
    const = lambda a: pl.BlockSpec(a.shape, lambda i, j: (0,) * a.ndim)
    cols = lambda width: pl.BlockSpec((d, width), lambda i, j: (0, j))
    perms = [_stride_permutation(dl) for dl in dil[1:]]
    gain2 = gain.reshape(1, d)
    ng = w_ba.shape[1]
    return pl.pallas_call(
        _in_proj_kernel,
        out_shape=[jax.ShapeDtypeStruct((dl, s // dl, PROJ_CHUNK * PROJ_STEPS), BF16) for dl in dil] + [
            jax.ShapeDtypeStruct((s, w_conv.shape[1]), BF16),
            jax.ShapeDtypeStruct((s, w_zg.shape[1]), BF16),
            jax.ShapeDtypeStruct((s, ng), F32)],
        grid=(s // TM_PROJ, PROJ_STEPS),
        in_specs=[pl.BlockSpec((TM_PROJ, d), lambda i, j: (i, 0)),
                  pl.BlockSpec((CONV_HALO, d), lambda i, j: (jnp.maximum(i * halo_blocks - 1, 0), 0)),
                  const(gain2), const(perms[0]), const(perms[1]),
                  cols(PROJ_CHUNK), cols(PROJ_CHUNK), cols(PROJ_CHUNK),
                  cols(2 * PROJ_CHUNK), cols(2 * PROJ_CHUNK), const(w_ba),
                  pl.BlockSpec((CONV_WIDTH, 2 * PROJ_CHUNK), lambda i, j: (0, j))],
        out_specs=[pl.BlockSpec((dl, TM_PROJ // dl, PROJ_CHUNK), lambda i, j: (0, i, j)) for dl in dil] + [
            pl.BlockSpec((TM_PROJ, 2 * PROJ_CHUNK), lambda i, j: (i, j)),
            pl.BlockSpec((TM_PROJ, 2 * PROJ_CHUNK), lambda i, j: (i, j)),
            pl.BlockSpec((TM_PROJ, ng), lambda i, j: (i, 0))],
        scratch_shapes=[pltpu.VMEM((TM_PROJ + CONV_HALO, d), BF16),
                        pltpu.VMEM((TM_PROJ, d), BF16),
                        pltpu.VMEM((TM_PROJ, d), BF16)],
        compiler_params=pltpu.CompilerParams(
            dimension_semantics=("parallel", "arbitrary"), vmem_limit_bytes=VMEM_LIMIT),
        name="in_proj",
    )(x, x, gain2, perms[0], perms[1], *w_attn, w_conv, w_zg, w_ba, conv_w)


def _attn_bias_table(group):
    _, dilation = ATTN_PATTERNS[group]
    heads = np.arange(group * HEADS_PER_GROUP, (group + 1) * HEADS_PER_GROUP, dtype=np.float32)
    slopes = np.exp2(np.float32(-8.0) * (heads + 1) / np.float32(N_HEADS_A)).astype(np.float32)
    i = np.arange(LANES)[:, None]
    c = np.arange(2 * LANES)[None, :]
    delta = LANES + i - c
    valid = (delta >= 0) & (delta <= ATTN_WINDOW_SUB)
    bias = -(slopes * np.float32(dilation))[:, None, None] * delta.astype(np.float32)[None]
    return jnp.asarray(np.where(valid[None], bias, -np.inf).astype(np.float32))


def _attn_kernel(q_ref, kc_ref, kp_ref, vc_ref, vp_ref, bias_ref, o_ref, lse_ref):
    first_tile = pl.program_id(1) == 0
    scale = HEAD_DIM ** -0.5
    lane = lax.broadcasted_iota(jnp.int32, (LANES, LANES), 1)
    key_col = lax.broadcasted_iota(jnp.int32, (LANES, 2 * LANES), 1)
    for b in range(TQ_ATTN // LANES):
        rows = slice(b * LANES, (b + 1) * LANES)
        lse_tile = jnp.zeros((LANES, LANES), F32)
        for j in range(HEADS_PER_GROUP):
            cols = slice(j * HEAD_DIM, (j + 1) * HEAD_DIM)
            q = q_ref[rows, cols]
            if b == 0:
                k2 = jnp.concatenate([kp_ref[:, cols], kc_ref[0:LANES, cols]], axis=0)
                v2 = jnp.concatenate([vp_ref[:, cols], vc_ref[0:LANES, cols]], axis=0)
            else:
                k2 = kc_ref[(b - 1) * LANES:(b + 1) * LANES, cols]
                v2 = vc_ref[(b - 1) * LANES:(b + 1) * LANES, cols]
            s = _dot_nt(q, k2) * scale + bias_ref[j]
            if b == 0:
                s = jnp.where(jnp.logical_and(first_tile, key_col < LANES), -jnp.inf, s)
            m = jnp.max(s, axis=-1, keepdims=True)
            p = jnp.exp(s - m)
            den = jnp.sum(p, axis=-1, keepdims=True)
            o_ref[rows, cols] = (_dot(p.astype(BF16), v2) / den).astype(o_ref.dtype)
            lse_tile = jnp.where(lane == j, m + jnp.log(den), lse_tile)
        lse_ref[rows, :] = lse_tile


def _attn_group(qkv, group):
    d, r, _ = qkv.shape
    halo_step = TQ_ATTN // LANES

    def cur(section):
        return pl.BlockSpec((None, TQ_ATTN, D_ATTN_OUT), lambda sub, n: (sub, n, section))

    def prev(section):
        return pl.BlockSpec((None, LANES, D_ATTN_OUT),
                            lambda sub, n: (sub, jnp.maximum(n * halo_step - 1, 0), section))

    return pl.pallas_call(
        _attn_kernel,
        out_shape=(jax.ShapeDtypeStruct((d, r, D_ATTN_OUT), BF16),
                   jax.ShapeDtypeStruct((d, r, LANES), F32)),
        grid=(d, r // TQ_ATTN),
        in_specs=[cur(0), cur(1), prev(1), cur(2), prev(2),
                  pl.BlockSpec((HEADS_PER_GROUP, LANES, 2 * LANES), lambda sub, n: (0, 0, 0))],
        out_specs=(pl.BlockSpec((None, TQ_ATTN, D_ATTN_OUT), lambda sub, n: (sub, n, 0)),
                   pl.BlockSpec((None, TQ_ATTN, LANES), lambda sub, n: (sub, n, 0))),
        compiler_params=pltpu.CompilerParams(
            dimension_semantics=("parallel", "arbitrary"), vmem_limit_bytes=VMEM_LIMIT),
        name=f"attn_group{group}",
    )(qkv, qkv, qkv, qkv, qkv, _attn_bias_table(group))


def _delta_kernel(q_ref, k_ref, v_ref, sz_ref, ba_ref, alog_ref, dtb_ref, dn_ref, tri_ref, out_ref,
                  state_ref, kbeta_ref, qd_ref, kd_ref, vb_ref, kbe_ref,
                  gc_ref, gl_ref, gct_ref, o_ref, b_ref, inv_ref, m_ref, qk_ref, ubar_ref, w_ref):
    c_len = DELTA_CHUNK
    n_heads = DELTA_HEADS_PER_STEP
    tile = pl.program_id(1)

    @pl.when(tile == 0)
    def _():
        state_ref[...] = jnp.zeros_like(state_ref)

    ba = ba_ref[...]
    lane = lax.broadcasted_iota(jnp.int32, ba.shape, 1)
    pre = ba + dtb_ref[...]
    softplus = jnp.maximum(pre, 0.0) + jnp.log(1.0 + jnp.exp(-jnp.abs(pre)))
    bg = jnp.where(lane < 8, _sigmoid(ba), -jnp.exp(alog_ref[...]) * softplus)
    hi = bg.astype(BF16)
    r1 = bg - hi.astype(F32)
    mid = r1.astype(BF16)
    lo = (r1 - mid.astype(F32)).astype(BF16)
    sums = _dot(tri_ref[...], jnp.concatenate([hi, mid, lo], axis=1))
    sums = sums[:, 0:LANES] + sums[:, LANES:2 * LANES] + sums[:, 2 * LANES:3 * LANES]
    gc = sums[0:DELTA_TILE]
    gl = sums[DELTA_TILE:2 * DELTA_TILE]
    gc_ref[...] = gc
    gl_ref[...] = gl
    for c in range(DELTA_TILE // c_len):
        gct_ref[c] = jnp.transpose(gc[c * c_len:(c + 1) * c_len, :])

    for h in range(n_heads):
        cols = slice(h * HEAD_DIM, (h + 1) * HEAD_DIM)
        full = (DELTA_TILE, HEAD_DIM)
        gch = gc[:, 8 + h:9 + h]
        beta = jnp.broadcast_to(bg[:, h:h + 1], full)
        e_gc = jnp.broadcast_to(jnp.exp(gch), full)
        e_rest = jnp.broadcast_to(jnp.exp(gl[:, 8 + h:9 + h] - gch), full)
        kn = k_ref[:, cols].astype(F32)
        kb = kn * beta
        qd_ref[:, cols] = (q_ref[:, cols].astype(F32) * e_gc).astype(BF16)
        kbeta_ref[:, cols] = kb.astype(BF16)
        kbe_ref[:, cols] = (kb * e_gc).astype(BF16)
        kd_ref[:, cols] = (kn * e_rest).astype(BF16)
        vb_ref[:, cols] = (v_ref[:, cols].astype(F32) * beta).astype(BF16)

    row = lax.broadcasted_iota(jnp.int32, (c_len, c_len), 0)
    col = lax.broadcasted_iota(jnp.int32, (c_len, c_len), 1)
    lower = row >= col
    strict = row > col
    eye = jnp.where(row == col, 1.0, 0.0).astype(F32)
    pair = jnp.right_shift(row, 1) == jnp.right_shift(col, 1)
    n_levels = c_len.bit_length() - 2

    n_chunks = DELTA_TILE // c_len
    items = [(c, h) for c in range(n_chunks) for h in range(n_heads)]

    def rows_of(c):
        return slice(c * c_len, (c + 1) * c_len)

    def cols_of(h):
        return slice(h * HEAD_DIM, (h + 1) * HEAD_DIM)

    for i, (c, h) in enumerate(items):
        r, cols = rows_of(c), cols_of(h)
        kq = _dot_nt(jnp.concatenate([kbeta_ref[r, cols], q_ref[r, cols]], axis=0), k_ref[r, cols])
        g_col = gc_ref[r, 8 + h:9 + h]
        g_row = gct_ref[c, 8 + h:9 + h, :]
        decay = jnp.exp(jnp.where(lower, g_col - g_row, -jnp.inf))
        p = jnp.where(strict, -(kq[0:c_len] * decay), 0.0)
        b_ref[i] = p.astype(BF16)
        inv_ref[i] = (eye + jnp.where(pair, p, 0.0)).astype(BF16)
        qk_ref[i] = (kq[c_len:2 * c_len] * decay).astype(BF16)

    def level(l, carry):
        lb = l + 1
        off = jnp.logical_and(jnp.right_shift(row, lb) == jnp.right_shift(col, lb) + 1,
                              jnp.bitwise_and(jnp.right_shift(col, lb), 1) == 0)
        for i in range(len(items)):
            m_ref[i] = jnp.where(off, _dot(b_ref[i], inv_ref[i]), 0.0).astype(BF16)
        for i in range(len(items)):
            d = inv_ref[i]
            inv_ref[i] = d + _dot(d, m_ref[i]).astype(BF16)
        return carry

    lax.fori_loop(0, n_levels, level, 0)

    for i, (c, h) in enumerate(items):
        r, cols = rows_of(c), cols_of(h)
        sol = _dot(inv_ref[i], jnp.concatenate([vb_ref[r, cols], kbe_ref[r, cols]], axis=1))
        ubar_ref[r, cols] = sol[:, 0:HEAD_DIM]
        w_ref[r, cols] = sol[:, HEAD_DIM:2 * HEAD_DIM].astype(BF16)

    for c in range(n_chunks):
        r = rows_of(c)
        ws = [_dot(jnp.concatenate([w_ref[r, cols_of(h)], qd_ref[r, cols_of(h)]], axis=0),
                   state_ref[h].astype(BF16)) for h in range(n_heads)]
        us = [(ubar_ref[r, cols_of(h)] - ws[h][0:c_len]).astype(BF16) for h in range(n_heads)]
        for h in range(n_heads):
            cols = cols_of(h)
            o_ref[r, cols] = ws[h][c_len:2 * c_len] + _dot(qk_ref[c * n_heads + h], us[h])
            g_last = jnp.exp(jnp.broadcast_to(gl_ref[r, 8 + h:9 + h], (c_len, HEAD_DIM))[0:1, :])
            state_ref[h] = (jnp.broadcast_to(g_last, (HEAD_DIM, HEAD_DIM)) * state_ref[h]
                            + _dot_tn(kd_ref[r, cols], us[h]))

    dn = dn_ref[...]
    for h in range(n_heads):
        cols = slice(h * HEAD_DIM, (h + 1) * HEAD_DIM)
        o = o_ref[:, cols]
        o = o * lax.rsqrt(jnp.mean(o * o, axis=-1, keepdims=True) + EPS) * dn
        out_ref[:, cols] = (o * sz_ref[:, cols].astype(F32)).astype(out_ref.dtype)


def _delta_tri_matrix():
    i = np.arange(DELTA_TILE)[:, None]
    t = np.arange(DELTA_TILE)[None, :]
    same = (i // DELTA_CHUNK) == (t // DELTA_CHUNK)
    return jnp.asarray(np.concatenate([same & (t <= i), same], axis=0).astype(np.float32), dtype=BF16)


def _delta_rule(qkv, zg, ba, a_log, dt_bias, dn_norm):
    s = qkv.shape[0]
    hps = DELTA_HEADS_PER_STEP
    width = hps * HEAD_DIM
    groups = N_HEADS_B // hps
    n_items = (DELTA_TILE // DELTA_CHUNK) * hps

    def rows(offset):
        return pl.BlockSpec((DELTA_TILE, width), lambda g, t: (t, offset + g))

    def lane_table(vals):
        tab = jnp.zeros((groups, LANES), F32).at[:, 8:8 + hps].set(vals.reshape(groups, hps).astype(F32))
        return tab.reshape(1, groups * LANES)

    bf = lambda: pltpu.VMEM((DELTA_TILE, width), BF16)
    return pl.pallas_call(
        _delta_kernel,
        out_shape=jax.ShapeDtypeStruct((s, D_DELTA), BF16),
        grid=(groups, s // DELTA_TILE),
        in_specs=[rows(0), rows(groups), rows(2 * groups), rows(0),
                  pl.BlockSpec((DELTA_TILE, LANES), lambda g, t: (t, g)),
                  pl.BlockSpec((1, LANES), lambda g, t: (0, g)),
                  pl.BlockSpec((1, LANES), lambda g, t: (0, g)),
                  pl.BlockSpec((1, HEAD_DIM), lambda g, t: (0, 0)),
                  pl.BlockSpec((2 * DELTA_TILE, DELTA_TILE), lambda g, t: (0, 0))],
        out_specs=pl.BlockSpec((DELTA_TILE, width), lambda g, t: (t, g)),
        scratch_shapes=[
            pltpu.VMEM((hps, HEAD_DIM, HEAD_DIM), F32),
            bf(), bf(), bf(), bf(), bf(),
            pltpu.VMEM((DELTA_TILE, LANES), F32),
            pltpu.VMEM((DELTA_TILE, LANES), F32),
            pltpu.VMEM((DELTA_TILE // DELTA_CHUNK, LANES, DELTA_CHUNK), F32),
            pltpu.VMEM((DELTA_TILE, width), F32),
            pltpu.VMEM((n_items, DELTA_CHUNK, DELTA_CHUNK), BF16),
            pltpu.VMEM((n_items, DELTA_CHUNK, DELTA_CHUNK), BF16),
            pltpu.VMEM((n_items, DELTA_CHUNK, DELTA_CHUNK), BF16),
            pltpu.VMEM((n_items, DELTA_CHUNK, DELTA_CHUNK), BF16),
            pltpu.VMEM((DELTA_TILE, width), F32),
            pltpu.VMEM((DELTA_TILE, width), BF16),
        ],
        compiler_params=pltpu.CompilerParams(
            dimension_semantics=("parallel", "arbitrary"), vmem_limit_bytes=VMEM_LIMIT),
        name="delta_rule",
    )(qkv, qkv, qkv, zg, ba, lane_table(a_log), lane_table(dt_bias),
      dn_norm.reshape(1, HEAD_DIM).astype(F32), _delta_tri_matrix())


def _merge_kernel(o0_ref, o1_ref, o2_ref, l0_ref, l1_ref, l2_ref, u1_ref, u2_ref, od_ref, ga_ref, gb_ref, x_ref,
                  wpa_ref, wpd_ref, wo_ref, out_ref):
    tm = out_ref.shape[0]

    def to_positions(o_ref, l_ref, u_ref):
        o = _dot(u_ref[...], o_ref[...].reshape(tm, D_ATTN_OUT))
        lse = l_ref[...].reshape(tm, LANES)
        hi = lse.astype(BF16)
        r1 = lse - hi.astype(F32)
        mid = r1.astype(BF16)
        lo = (r1 - mid.astype(F32)).astype(BF16)
        parts = _dot(u_ref[...], jnp.concatenate([hi, mid, lo], axis=1))
        return o, parts[:, 0:LANES] + parts[:, LANES:2 * LANES] + parts[:, 2 * LANES:3 * LANES]

    o0 = o0_ref[...].reshape(tm, D_ATTN_OUT).astype(F32)
    l0 = l0_ref[...].reshape(tm, LANES)
    o1, l1 = to_positions(o1_ref, l1_ref, u1_ref)
    o2, l2 = to_positions(o2_ref, l2_ref, u2_ref)
    mx = jnp.maximum(jnp.maximum(l0, l1), l2)
    e0, e1, e2 = jnp.exp(l0 - mx), jnp.exp(l1 - mx), jnp.exp(l2 - mx)
    inv = 1.0 / (e0 + e1 + e2)
    w0, w1, w2 = e0 * inv, e1 * inv, e2 * inv
    parts = []
    for j in range(HEADS_PER_GROUP):
        cols = slice(j * HEAD_DIM, (j + 1) * HEAD_DIM)
        parts.append(w0[:, j:j + 1] * o0[:, cols] + w1[:, j:j + 1] * o1[:, cols] + w2[:, j:j + 1] * o2[:, cols])
    y_a = _dot(jnp.concatenate(parts, axis=1).astype(BF16), wpa_ref[...])
    y_b = _dot(od_ref[...], wpd_ref[...])
    merged = ga_ref[...].astype(F32) * y_a + gb_ref[...].astype(F32) * y_b
    out_ref[...] = x_ref[...] + _dot(merged.astype(BF16), wo_ref[...])


def _position_permutation(dilation, tm):
    p = np.arange(tm)
    src = (p % dilation) * (tm // dilation) + p // dilation
    return jnp.asarray((src[:, None] == np.arange(tm)[None, :]).astype(np.float32), dtype=BF16)


def _merge_proj(outs, lses, o_delta, zg, x, w_pa, w_pd, w_o):
    s, d = x.shape
    tm = TM_MERGE
    row = lambda width: pl.BlockSpec((tm, width), lambda i: (i, 0))
    full = lambda a: pl.BlockSpec(a.shape, lambda i: (0, 0))
    grouped = lambda a: pl.BlockSpec((a.shape[0], tm // a.shape[0], a.shape[2]), lambda i: (0, i, 0))
    unperm = [_position_permutation(o.shape[0], tm) for o in outs[1:]]
    return pl.pallas_call(
        _merge_kernel,
        out_shape=jax.ShapeDtypeStruct((s, d), F32),
        grid=(s // tm,),
        in_specs=[grouped(o) for o in outs] + [grouped(l) for l in lses] + [full(u) for u in unperm] + [
            row(D_DELTA),
            pl.BlockSpec((tm, d), lambda i: (i, 1)),
            pl.BlockSpec((tm, d), lambda i: (i, 2)),
            row(d), full(w_pa), full(w_pd), full(w_o)],
        out_specs=row(d),
        compiler_params=pltpu.CompilerParams(
            dimension_semantics=("parallel",), vmem_limit_bytes=VMEM_LIMIT),
        name="merge_proj",
    )(*outs, *lses, *unperm, o_delta, zg, zg, x, w_pa, w_pd, w_o)


def _ffn_kernel(h_ref, g_ref, wg_ref, wu_ref, wd_ref, gf_ref, o_ref, *, final_norm):
    h = h_ref[...]
    hn = (h * lax.rsqrt(jnp.mean(h * h, axis=-1, keepdims=True) + EPS) * g_ref[...]).astype(BF16)
    act = (_silu(_dot(hn, wg_ref[...])) * _dot(hn, wu_ref[...])).astype(BF16)
    y = h + _dot(act, wd_ref[...])
    if final_norm:
        y = y * lax.rsqrt(jnp.mean(y * y, axis=-1, keepdims=True) + EPS) * gf_ref[...]
    o_ref[...] = y


def _ffn(h, gain, w_g, w_u, w_d, gain_final, final_norm):
    s, d = h.shape
    tm = TM_FFN
    once = lambda a: pl.BlockSpec(a.shape, lambda i: (0, 0), pipeline_mode=pl.Buffered(1))
    vec = pl.BlockSpec((1, d), lambda i: (0, 0))
    return pl.pallas_call(
        functools.partial(_ffn_kernel, final_norm=final_norm),
        out_shape=jax.ShapeDtypeStruct((s, d), F32),
        grid=(s // tm,),
        in_specs=[pl.BlockSpec((tm, d), lambda i: (i, 0)), vec, once(w_g), once(w_u), once(w_d), vec],
        out_specs=pl.BlockSpec((tm, d), lambda i: (i, 0)),
        compiler_params=pltpu.CompilerParams(
            dimension_semantics=("parallel",), vmem_limit_bytes=VMEM_LIMIT),
        name="ffn",
    )(h, gain.reshape(1, d), w_g, w_u, w_d, gain_final.reshape(1, d))


def _gate_weight_columns(w_in, col_beta, col_a):
    hps = DELTA_HEADS_PER_STEP
    groups = N_HEADS_B // hps
    d = w_in.shape[0]
    w = jnp.zeros((d, groups, LANES), w_in.dtype)
    w = w.at[:, :, 0:hps].set(w_in[:, col_beta:col_beta + N_HEADS_B].reshape(d, groups, hps))
    w = w.at[:, :, 8:8 + hps].set(w_in[:, col_a:col_a + N_HEADS_B].reshape(d, groups, hps))
    return w.reshape(d, groups * LANES)


def kernel(x, norm_mix, w_in, conv_w, a_log, dt_bias, dn_norm, w_proj_attn, w_proj_delta,
           w_out, norm_ffn, w_gate, w_up, w_down, norm_final):
    b, s, d = x.shape
    assert b == 1 and s % (TM_PROJ * 16) == 0 and d == 8 * LANES
    depth = w_in.shape[0]
    c_attn = 3 * D_ATTN
    c_z = c_attn + 4 * D_DELTA
    c_beta, c_a = c_z, c_z + N_HEADS_B
    c_gate = c_z + 2 * N_HEADS_B
    h = x.reshape(s, d)
    for layer in range(depth):
        wl = w_in[layer]
        c_dq = c_attn
        w_attn = [jnp.concatenate([wl[:, sec * D_ATTN + g * D_ATTN_OUT:sec * D_ATTN + (g + 1) * D_ATTN_OUT]
                                   for sec in range(3)], axis=1).astype(BF16) for g in range(len(ATTN_PATTERNS))]
        w_conv = wl[:, c_dq:c_dq + 3 * D_DELTA].astype(BF16)
        w_zg = jnp.concatenate([wl[:, c_dq + 3 * D_DELTA:c_z], wl[:, c_gate:]], axis=1).astype(BF16)
        w_ba = _gate_weight_columns(wl, c_beta, c_a).astype(BF16)
        *qkv_attn, qkv_d, zg, ba = _in_proj(h, norm_mix[layer], w_attn, w_conv, w_zg, w_ba, conv_w[layer])

        outs, lses = zip(*[_attn_group(qkv_g, g) for g, qkv_g in enumerate(qkv_attn)])
        o_delta = _delta_rule(qkv_d, zg, ba, a_log[layer], dt_bias[layer], dn_norm[layer])
        h = _merge_proj(outs, lses, o_delta, zg, h, w_proj_attn[layer].astype(BF16),
                        w_proj_delta[layer].astype(BF16), w_out[layer].astype(BF16))
        h = _ffn(h, norm_ffn[layer], w_gate[layer].astype(BF16), w_up[layer].astype(BF16),
                 w_down[layer].astype(BF16), norm_final, layer == depth - 1)
    return h.reshape(b, s, d)
```

```python
import functools

import numpy as np
import jax
import jax.numpy as jnp
from jax import lax
from jax.experimental import pallas as pl
from jax.experimental.pallas import tpu as pltpu

F32 = jnp.float32
BF16 = jnp.bfloat16

EPS = 1e-6
LOG2_E = 1.4426950408889634
LN_2 = 0.6931471805599453
LANES = 128
HEAD_DIM = 128
ATTN_PATTERNS = ((128, 1), (512, 4), (2048, 16))
HEADS_PER_GROUP = 4
N_HEADS_A = len(ATTN_PATTERNS) * HEADS_PER_GROUP
D_ATTN = N_HEADS_A * HEAD_DIM
D_ATTN_OUT = HEADS_PER_GROUP * HEAD_DIM
ATTN_WINDOW_SUB = 128
N_HEADS_B = 8
D_DELTA = N_HEADS_B * HEAD_DIM
CONV_WIDTH = 4

VMEM_LIMIT = 56 * 1024 * 1024

TM_PROJ = 1024
CONV_HALO = 16
PROJ_CHUNK = 256
PROJ_STEPS = 6
PERM_ROWS = 256
TQ_ATTN = 1024
DELTA_CHUNK = 128
DELTA_TILE = 512
DELTA_HEADS_PER_STEP = 8
TM_MERGE = 512
TM_FFN = 512


def _sigmoid(v):
    return 1.0 / (1.0 + jnp.exp(-v))


def _silu(v):
    return v * _sigmoid(v)


def _dot(a, b):
    return jnp.dot(a, b, preferred_element_type=F32)


def _dot_nt(a, b):
    return lax.dot_general(a, b, (((1,), (1,)), ((), ())), preferred_element_type=F32)


def _dot_tn(a, b):
    return lax.dot_general(a, b, (((0,), (0,)), ((), ())), preferred_element_type=F32)


def _stride_permutation(dilation):
    piece = PERM_ROWS // dilation
    i = np.arange(PERM_ROWS)
    src = (i % piece) * dilation + i // piece
    return jnp.asarray((src[:, None] == np.arange(PERM_ROWS)[None, :]).astype(np.float32), dtype=BF16)


def _in_proj_kernel(x_ref, xh_ref, g_ref, p1_ref, p2_ref, wa0_ref, wa1_ref, wa2_ref, wc_ref, wz_ref, wba_ref,
                    cw_ref, a0_ref, a1_ref, a2_ref, c_ref, z_ref, ba_ref, xn_ref, xn1_ref, xn2_ref):
    i, j = pl.program_id(0), pl.program_id(1)
    tm = TM_PROJ
    chunk = PROJ_CHUNK

    def norm(x):
        return (x * lax.rsqrt(jnp.mean(x * x, axis=-1, keepdims=True) + EPS) * g_ref[...]).astype(BF16)

    @pl.when(j == 0)
    def _():
        xn = norm(x_ref[...])
        xn_ref[0:tm, :] = xn
        xn_ref[tm:tm + CONV_HALO, :] = jnp.where(i == 0, jnp.zeros((), BF16), norm(xh_ref[...]))
        ba_ref[...] = _dot(xn, wba_ref[...])
        for perm_ref, dst_ref, (_, dilation) in ((p1_ref, xn1_ref, ATTN_PATTERNS[1]),
                                                 (p2_ref, xn2_ref, ATTN_PATTERNS[2])):
            rows = tm // dilation
            piece = PERM_ROWS // dilation
            for g in range(tm // PERM_ROWS):
                y = _dot(perm_ref[...], xn[g * PERM_ROWS:(g + 1) * PERM_ROWS, :]).astype(BF16)
                for sub in range(dilation):
                    dst_ref[sub * rows + g * piece:sub * rows + (g + 1) * piece, :] = (
                        y[sub * piece:(sub + 1) * piece, :])

    def attn_store(acc, o_ref):
        o_ref[...] = acc.astype(BF16).reshape(o_ref.shape)

    def conv_store(acc, c0):
        x = acc[0:tm]
        halo = acc[tm + CONV_HALO - 8:tm + CONV_HALO]
        w = cw_ref[:, c0:c0 + chunk]
        row8 = lax.broadcasted_iota(jnp.int32, halo.shape, 0)
        y = w[CONV_WIDTH - 1:CONV_WIDTH, :] * x
        for s in range(1, CONV_WIDTH):
            xs = pltpu.roll(x, s, 0)
            head = jnp.where(row8 < s, pltpu.roll(halo, s, 0), xs[0:8])
            xs = jnp.concatenate([head, xs[8:]], axis=0)
            y = y + w[CONV_WIDTH - 1 - s:CONV_WIDTH - s, :] * xs
        y = y * _sigmoid(y)
        q_scale = jnp.where(j < PROJ_STEPS // 3, HEAD_DIM ** -0.5, 1.0)
        for h in range(chunk // HEAD_DIM):
            seg = y[:, h * HEAD_DIM:(h + 1) * HEAD_DIM]
            inv = lax.rsqrt(jnp.sum(seg * seg, axis=-1, keepdims=True) + EPS) * q_scale
            factor = jnp.where(j < 2 * PROJ_STEPS // 3, inv, 1.0)
            c_ref[:, c0 + h * HEAD_DIM:c0 + (h + 1) * HEAD_DIM] = (seg * factor).astype(BF16)

    def gate_store(acc, c0):
        sig = _sigmoid(acc)
        z_ref[:, c0:c0 + chunk] = (sig * jnp.where(j < PROJ_STEPS // 3, acc, 1.0)).astype(BF16)

    acc_a0 = _dot(xn_ref[0:tm, :], wa0_ref[...])
    acc_c0 = _dot(xn_ref[...], wc_ref[:, 0:chunk])
    attn_store(acc_a0, a0_ref)
    acc_a1 = _dot(xn1_ref[...], wa1_ref[...])
    conv_store(acc_c0, 0)
    acc_z0 = _dot(xn_ref[0:tm, :], wz_ref[:, 0:chunk])
    attn_store(acc_a1, a1_ref)
    acc_c1 = _dot(xn_ref[...], wc_ref[:, chunk:2 * chunk])
    gate_store(acc_z0, 0)
    acc_a2 = _dot(xn2_ref[...], wa2_ref[...])
    conv_store(acc_c1, chunk)
    acc_z1 = _dot(xn_ref[0:tm, :], wz_ref[:, chunk:2 * chunk])
    attn_store(acc_a2, a2_ref)
    gate_store(acc_z1, chunk)


def _in_proj(x, gain, w_attn, w_conv, w_zg, w_ba, conv_w):
    s, d = x.shape
    dil = [p[1] for p in ATTN_PATTERNS]
    assert dil[0] == 1 and w_conv.shape[1] == 2 * PROJ_CHUNK * PROJ_STEPS == w_zg.shape[1]
    assert all(w.shape[1] == PROJ_CHUNK * PROJ_STEPS for w in w_attn)
    halo_blocks = TM_PROJ // CONV_HALO
    const = lambda a: pl.BlockSpec(a.shape, lambda i, j: (0,) * a.ndim)
    cols = lambda width: pl.BlockSpec((d, width), lambda i, j: (0, j))
    perms = [_stride_permutation(dl) for dl in dil[1:]]
    gain2 = gain.reshape(1, d)
    ng = w_ba.shape[1]
    return pl.pallas_call(
        _in_proj_kernel,
        out_shape=[jax.ShapeDtypeStruct((dl, s // dl, PROJ_CHUNK * PROJ_STEPS), BF16) for dl in dil] + [
            jax.ShapeDtypeStruct((s, w_conv.shape[1]), BF16),
            jax.ShapeDtypeStruct((s, w_zg.shape[1]), BF16),
            jax.ShapeDtypeStruct((s, ng), F32)],
        grid=(s // TM_PROJ, PROJ_STEPS),
        in_specs=[pl.BlockSpec((TM_PROJ, d), lambda i, j: (i, 0)),
                  pl.BlockSpec((CONV_HALO, d), lambda i, j: (jnp.maximum(i * halo_blocks - 1, 0), 0)),
                  const(gain2), const(perms[0]), const(perms[1]),
                  cols(PROJ_CHUNK), cols(PROJ_CHUNK), cols(PROJ_CHUNK),
                  cols(2 * PROJ_CHUNK), cols(2 * PROJ_CHUNK), const(w_ba),
                  pl.BlockSpec((CONV_WIDTH, 2 * PROJ_CHUNK), lambda i, j: (0, j))],
        out_specs=[pl.BlockSpec((dl, TM_PROJ // dl, PROJ_CHUNK), lambda i, j: (0, i, j)) for dl in dil] + [
            pl.BlockSpec((TM_PROJ, 2 * PROJ_CHUNK), lambda i, j: (i, j)),
            pl.BlockSpec((TM_PROJ, 2 * PROJ_CHUNK), lambda i, j: (i, j)),
            pl.BlockSpec((TM_PROJ, ng), lambda i, j: (i, 0))],
        scratch_shapes=[pltpu.VMEM((TM_PROJ + CONV_HALO, d), BF16),
                        pltpu.VMEM((TM_PROJ, d), BF16),
                        pltpu.VMEM((TM_PROJ, d), BF16)],
        compiler_params=pltpu.CompilerParams(
            dimension_semantics=("parallel", "arbitrary"), vmem_limit_bytes=VMEM_LIMIT),
        name="in_proj",
    )(x, x, gain2, perms[0], perms[1], *w_attn, w_conv, w_zg, w_ba, conv_w)


def _attn_bias_table(group):
    _, dilation = ATTN_PATTERNS[group]
    heads = np.arange(group * HEADS_PER_GROUP, (group + 1) * HEADS_PER_GROUP, dtype=np.float32)
    slopes = np.exp2(np.float32(-8.0) * (heads + 1) / np.float32(N_HEADS_A)).astype(np.float32)
    i = np.arange(LANES)[:, None]
    c = np.arange(2 * LANES)[None, :]
    delta = LANES + i - c
    valid = (delta >= 0) & (delta <= ATTN_WINDOW_SUB)
    bias = -(slopes * np.float32(dilation))[:, None, None] * delta.astype(np.float32)[None]
    return jnp.asarray(np.where(valid[None], bias * np.float32(LOG2_E), -np.inf).astype(np.float32))


def _attn_kernel(q_ref, kc_ref, kp_ref, vc_ref, vp_ref, bias_ref, o_ref, lse_ref):
    first_tile = pl.program_id(1) == 0
    scale2 = (HEAD_DIM ** -0.5) * LOG2_E
    lane = lax.broadcasted_iota(jnp.int32, (LANES, LANES), 1)
    key_col = lax.broadcasted_iota(jnp.int32, (LANES, 2 * LANES), 1)
    ones = jnp.ones((2 * LANES, HEAD_DIM), BF16)
    for b in range(TQ_ATTN // LANES):
        rows = slice(b * LANES, (b + 1) * LANES)
        lse_tile = jnp.zeros((LANES, LANES), F32)
        for j in range(HEADS_PER_GROUP):
            cols = slice(j * HEAD_DIM, (j + 1) * HEAD_DIM)
            q = q_ref[rows, cols]
            if b == 0:
                k2 = jnp.concatenate([kp_ref[:, cols], kc_ref[0:LANES, cols]], axis=0)
                v2 = jnp.concatenate([vp_ref[:, cols], vc_ref[0:LANES, cols]], axis=0)
            else:
                k2 = kc_ref[(b - 1) * LANES:(b + 1) * LANES, cols]
                v2 = vc_ref[(b - 1) * LANES:(b + 1) * LANES, cols]
            s = _dot_nt(q, k2) * scale2 + bias_ref[j]
            if b == 0:
                s = jnp.where(jnp.logical_and(first_tile, key_col < LANES), -jnp.inf, s)
            m = jnp.max(s, axis=-1, keepdims=True)
            p = jnp.exp2(s - m)
            pv = _dot(p.astype(BF16), jnp.concatenate([v2, ones], axis=1))
            den = pv[:, HEAD_DIM:2 * HEAD_DIM]
            o_ref[rows, cols] = (pv[:, 0:HEAD_DIM] / den).astype(o_ref.dtype)
            lse_tile = jnp.where(lane == j, (m + jnp.log2(den)) * LN_2, lse_tile)
        lse_ref[rows, :] = lse_tile


def _attn_group(qkv, group):
    d, r, _ = qkv.shape
    halo_step = TQ_ATTN // LANES

    def cur(section):
        return pl.BlockSpec((None, TQ_ATTN, D_ATTN_OUT), lambda sub, n: (sub, n, section))

    def prev(section):
        return pl.BlockSpec((None, LANES, D_ATTN_OUT),
                            lambda sub, n: (sub, jnp.maximum(n * halo_step - 1, 0), section))

    return pl.pallas_call(
        _attn_kernel,
        out_shape=(jax.ShapeDtypeStruct((d, r, D_ATTN_OUT), BF16),
                   jax.ShapeDtypeStruct((d, r, LANES), F32)),
        grid=(d, r // TQ_ATTN),
        in_specs=[cur(0), cur(1), prev(1), cur(2), prev(2),
                  pl.BlockSpec((HEADS_PER_GROUP, LANES, 2 * LANES), lambda sub, n: (0, 0, 0))],
        out_specs=(pl.BlockSpec((None, TQ_ATTN, D_ATTN_OUT), lambda sub, n: (sub, n, 0)),
                   pl.BlockSpec((None, TQ_ATTN, LANES), lambda sub, n: (sub, n, 0))),
        compiler_params=pltpu.CompilerParams(
            dimension_semantics=("parallel", "arbitrary"), vmem_limit_bytes=VMEM_LIMIT),
        name=f"attn_group{group}",
    )(qkv, qkv, qkv, qkv, qkv, _attn_bias_table(group))


def _delta_kernel(q_ref, k_ref, v_ref, sz_ref, ba_ref, alog_ref, dtb_ref, dn_ref, tri_ref, out_ref,
                  state_ref, kbeta_ref, qd_ref, kd_ref, vb_ref, kbe_ref,
                  gc_ref, gl_ref, gct_ref, o_ref, b_ref, inv_ref, m_ref, qk_ref, ubar_ref, w_ref):
    c_len = DELTA_CHUNK
    n_heads = DELTA_HEADS_PER_STEP
    tile = pl.program_id(1)

    @pl.when(tile == 0)
    def _():
        state_ref[...] = jnp.zeros_like(state_ref)

    ba = ba_ref[...]
    lane = lax.broadcasted_iota(jnp.int32, ba.shape, 1)
    pre = ba + dtb_ref[...]
    softplus = jnp.maximum(pre, 0.0) + jnp.log(1.0 + jnp.exp(-jnp.abs(pre)))
    bg = jnp.where(lane < 8, _sigmoid(ba), -jnp.exp(alog_ref[...]) * softplus)
    hi = bg.astype(BF16)
    r1 = bg - hi.astype(F32)
    mid = r1.astype(BF16)
    lo = (r1 - mid.astype(F32)).astype(BF16)
    sums = _dot(tri_ref[...], jnp.concatenate([hi, mid, lo], axis=1))
    sums = sums[:, 0:LANES] + sums[:, LANES:2 * LANES] + sums[:, 2 * LANES:3 * LANES]
    gc = sums[0:DELTA_TILE]
    gl = sums[DELTA_TILE:2 * DELTA_TILE]
    gc_ref[...] = gc
    gl_ref[...] = gl
    for c in range(DELTA_TILE // c_len):
        gct_ref[c] = jnp.transpose(gc[c * c_len:(c + 1) * c_len, :])

    for h in range(n_heads):
        cols = slice(h * HEAD_DIM, (h + 1) * HEAD_DIM)
        full = (DELTA_TILE, HEAD_DIM)
        gch = gc[:, 8 + h:9 + h]
        beta = jnp.broadcast_to(bg[:, h:h + 1], full)
        e_gc = jnp.broadcast_to(jnp.exp(gch), full)
        e_rest = jnp.broadcast_to(jnp.exp(gl[:, 8 + h:9 + h] - gch), full)
        kn = k_ref[:, cols].astype(F32)
        kb = kn * beta
        qd_ref[:, cols] = (q_ref[:, cols].astype(F32) * e_gc).astype(BF16)
        kbeta_ref[:, cols] = kb.astype(BF16)
        kbe_ref[:, cols] = (kb * e_gc).astype(BF16)
        kd_ref[:, cols] = (kn * e_rest).astype(BF16)
        vb_ref[:, cols] = (v_ref[:, cols].astype(F32) * beta).astype(BF16)

    row = lax.broadcasted_iota(jnp.int32, (c_len, c_len), 0)
    col = lax.broadcasted_iota(jnp.int32, (c_len, c_len), 1)
    lower = row >= col
    strict = row > col
    eye = jnp.where(row == col, 1.0, 0.0).astype(F32)
    pair = jnp.right_shift(row, 1) == jnp.right_shift(col, 1)
    n_levels = c_len.bit_length() - 2

    n_chunks = DELTA_TILE // c_len
    items = [(c, h) for c in range(n_chunks) for h in range(n_heads)]

    def rows_of(c):
        return slice(c * c_len, (c + 1) * c_len)

    def cols_of(h):
        return slice(h * HEAD_DIM, (h + 1) * HEAD_DIM)

    for i, (c, h) in enumerate(items):
        r, cols = rows_of(c), cols_of(h)
        kq = _dot_nt(jnp.concatenate([kbeta_ref[r, cols], q_ref[r, cols]], axis=0), k_ref[r, cols])
        g_col = gc_ref[r, 8 + h:9 + h]
        g_row = gct_ref[c, 8 + h:9 + h, :]
        decay = jnp.exp(jnp.where(lower, g_col - g_row, -jnp.inf))
        p = jnp.where(strict, -(kq[0:c_len] * decay), 0.0)
        b_ref[i] = p.astype(BF16)
        inv_ref[i] = (eye + jnp.where(pair, p, 0.0)).astype(BF16)
        qk_ref[i] = (kq[c_len:2 * c_len] * decay).astype(BF16)

    def level(l, carry):
        lb = l + 1
        off = jnp.logical_and(jnp.right_shift(row, lb) == jnp.right_shift(col, lb) + 1,
                              jnp.bitwise_and(jnp.right_shift(col, lb), 1) == 0)
        for i in range(len(items)):
            m_ref[i] = jnp.where(off, _dot(b_ref[i], inv_ref[i]), 0.0).astype(BF16)
        for i in range(len(items)):
            d = inv_ref[i]
            inv_ref[i] = d + _dot(d, m_ref[i]).astype(BF16)
        return carry

    lax.fori_loop(0, n_levels, level, 0)

    for i, (c, h) in enumerate(items):
        r, cols = rows_of(c), cols_of(h)
        sol = _dot(inv_ref[i], jnp.concatenate([vb_ref[r, cols], kbe_ref[r, cols]], axis=1))
        ubar_ref[r, cols] = sol[:, 0:HEAD_DIM]
        w_ref[r, cols] = sol[:, HEAD_DIM:2 * HEAD_DIM].astype(BF16)

    for c in range(n_chunks):
        r = rows_of(c)
        ws = [_dot(jnp.concatenate([w_ref[r, cols_of(h)], qd_ref[r, cols_of(h)]], axis=0),
                   state_ref[h].astype(BF16)) for h in range(n_heads)]
        us = [(ubar_ref[r, cols_of(h)] - ws[h][0:c_len]).astype(BF16) for h in range(n_heads)]
        for h in range(n_heads):
            cols = cols_of(h)
            o_ref[r, cols] = ws[h][c_len:2 * c_len] + _dot(qk_ref[c * n_heads + h], us[h])
            g_last = jnp.exp(jnp.broadcast_to(gl_ref[r, 8 + h:9 + h], (c_len, HEAD_DIM))[0:1, :])
            state_ref[h] = (jnp.broadcast_to(g_last, (HEAD_DIM, HEAD_DIM)) * state_ref[h]
                            + _dot_tn(kd_ref[r, cols], us[h]))

    dn = dn_ref[...]
    for h in range(n_heads):
        cols = slice(h * HEAD_DIM, (h + 1) * HEAD_DIM)
        o = o_ref[:, cols]
        o = o * lax.rsqrt(jnp.mean(o * o, axis=-1, keepdims=True) + EPS) * dn
        out_ref[:, cols] = (o * sz_ref[:, cols].astype(F32)).astype(out_ref.dtype)


def _delta_tri_matrix():
    i = np.arange(DELTA_TILE)[:, None]
    t = np.arange(DELTA_TILE)[None, :]
    same = (i // DELTA_CHUNK) == (t // DELTA_CHUNK)
    return jnp.asarray(np.concatenate([same & (t <= i), same], axis=0).astype(np.float32), dtype=BF16)


def _delta_rule(qkv, zg, ba, a_log, dt_bias, dn_norm):
    s = qkv.shape[0]
    hps = DELTA_HEADS_PER_STEP
    width = hps * HEAD_DIM
    groups = N_HEADS_B // hps
    n_items = (DELTA_TILE // DELTA_CHUNK) * hps

    def rows(offset):
        return pl.BlockSpec((DELTA_TILE, width), lambda g, t: (t, offset + g))

    def lane_table(vals):
        tab = jnp.zeros((groups, LANES), F32).at[:, 8:8 + hps].set(vals.reshape(groups, hps).astype(F32))
        return tab.reshape(1, groups * LANES)

    bf = lambda: pltpu.VMEM((DELTA_TILE, width), BF16)
    return pl.pallas_call(
        _delta_kernel,
        out_shape=jax.ShapeDtypeStruct((s, D_DELTA), BF16),
        grid=(groups, s // DELTA_TILE),
        in_specs=[rows(0), rows(groups), rows(2 * groups), rows(0),
                  pl.BlockSpec((DELTA_TILE, LANES), lambda g, t: (t, g)),
                  pl.BlockSpec((1, LANES), lambda g, t: (0, g)),
                  pl.BlockSpec((1, LANES), lambda g, t: (0, g)),
                  pl.BlockSpec((1, HEAD_DIM), lambda g, t: (0, 0)),
                  pl.BlockSpec((2 * DELTA_TILE, DELTA_TILE), lambda g, t: (0, 0))],
        out_specs=pl.BlockSpec((DELTA_TILE, width), lambda g, t: (t, g)),
        scratch_shapes=[
            pltpu.VMEM((hps, HEAD_DIM, HEAD_DIM), F32),
            bf(), bf(), bf(), bf(), bf(),
            pltpu.VMEM((DELTA_TILE, LANES), F32),
            pltpu.VMEM((DELTA_TILE, LANES), F32),
            pltpu.VMEM((DELTA_TILE // DELTA_CHUNK, LANES, DELTA_CHUNK), F32),
            pltpu.VMEM((DELTA_TILE, width), F32),
            pltpu.VMEM((n_items, DELTA_CHUNK, DELTA_CHUNK), BF16),
            pltpu.VMEM((n_items, DELTA_CHUNK, DELTA_CHUNK), BF16),
            pltpu.VMEM((n_items, DELTA_CHUNK, DELTA_CHUNK), BF16),
            pltpu.VMEM((n_items, DELTA_CHUNK, DELTA_CHUNK), BF16),
            pltpu.VMEM((DELTA_TILE, width), F32),
            pltpu.VMEM((DELTA_TILE, width), BF16),
        ],
        compiler_params=pltpu.CompilerParams(
            dimension_semantics=("parallel", "arbitrary"), vmem_limit_bytes=VMEM_LIMIT),
        name="delta_rule",
    )(qkv, qkv, qkv, zg, ba, lane_table(a_log), lane_table(dt_bias),
      dn_norm.reshape(1, HEAD_DIM).astype(F32), _delta_tri_matrix())


def _merge_kernel(o0_ref, o1_ref, o2_ref, l0_ref, l1_ref, l2_ref, u1_ref, u2_ref, od_ref, ga_ref, gb_ref, x_ref,
                  wpa_ref, wpd_ref, wo_ref, out_ref):
    tm = out_ref.shape[0]

    def to_positions(o_ref, l_ref, u_ref):
        o = _dot(u_ref[...], o_ref[...].reshape(tm, D_ATTN_OUT))
        lse = l_ref[...].reshape(tm, LANES)
        hi = lse.astype(BF16)
        r1 = lse - hi.astype(F32)
        mid = r1.astype(BF16)
        lo = (r1 - mid.astype(F32)).astype(BF16)
        parts = _dot(u_ref[...], jnp.concatenate([hi, mid, lo], axis=1))
        return o, parts[:, 0:LANES] + parts[:, LANES:2 * LANES] + parts[:, 2 * LANES:3 * LANES]

    o0 = o0_ref[...].reshape(tm, D_ATTN_OUT).astype(F32)
    l0 = l0_ref[...].reshape(tm, LANES)
    o1, l1 = to_positions(o1_ref, l1_ref, u1_ref)
    o2, l2 = to_positions(o2_ref, l2_ref, u2_ref)
    mx = jnp.maximum(jnp.maximum(l0, l1), l2)
    e0, e1, e2 = jnp.exp(l0 - mx), jnp.exp(l1 - mx), jnp.exp(l2 - mx)
    inv = 1.0 / (e0 + e1 + e2)
    w0, w1, w2 = e0 * inv, e1 * inv, e2 * inv
    half = tm // 2
    for r0 in (0, half):
        r = slice(r0, r0 + half)
        parts = []
        for j in range(HEADS_PER_GROUP):
            cols = slice(j * HEAD_DIM, (j + 1) * HEAD_DIM)
            parts.append(w0[r, j:j + 1] * o0[r, cols] + w1[r, j:j + 1] * o1[r, cols] + w2[r, j:j + 1] * o2[r, cols])
        y_a = _dot(jnp.concatenate(parts, axis=1).astype(BF16), wpa_ref[...])
        y_b = _dot(od_ref[r, :], wpd_ref[...])
        merged = ga_ref[r, :].astype(F32) * y_a + gb_ref[r, :].astype(F32) * y_b
        out_ref[r, :] = x_ref[r, :] + _dot(merged.astype(BF16), wo_ref[...])


def _position_permutation(dilation, tm):
    p = np.arange(tm)
    src = (p % dilation) * (tm // dilation) + p // dilation
    return jnp.asarray((src[:, None] == np.arange(tm)[None, :]).astype(np.float32), dtype=BF16)


def _merge_proj(outs, lses, o_delta, zg, x, w_pa, w_pd, w_o):
    s, d = x.shape
    tm = TM_MERGE
    row = lambda width: pl.BlockSpec((tm, width), lambda i: (i, 0))
    full = lambda a: pl.BlockSpec(a.shape, lambda i: (0, 0))
    grouped = lambda a: pl.BlockSpec((a.shape[0], tm // a.shape[0], a.shape[2]), lambda i: (0, i, 0))
    unperm = [_position_permutation(o.shape[0], tm) for o in outs[1:]]
    return pl.pallas_call(
        _merge_kernel,
        out_shape=jax.ShapeDtypeStruct((s, d), F32),
        grid=(s // tm,),
        in_specs=[grouped(o) for o in outs] + [grouped(l) for l in lses] + [full(u) for u in unperm] + [
            row(D_DELTA),
            pl.BlockSpec((tm, d), lambda i: (i, 1)),
            pl.BlockSpec((tm, d), lambda i: (i, 2)),
            row(d), full(w_pa), full(w_pd), full(w_o)],
        out_specs=row(d),
        compiler_params=pltpu.CompilerParams(
            dimension_semantics=("parallel",), vmem_limit_bytes=VMEM_LIMIT),
        name="merge_proj",
    )(*outs, *lses, *unperm, o_delta, zg, zg, x, w_pa, w_pd, w_o)


def _ffn_kernel(h_ref, g_ref, wg_ref, wu_ref, wd_ref, gf_ref, o_ref, *, final_norm):
    h = h_ref[...]
    hn = (h * lax.rsqrt(jnp.mean(h * h, axis=-1, keepdims=True) + EPS) * g_ref[...]).astype(BF16)
    act = (_silu(_dot(hn, wg_ref[...])) * _dot(hn, wu_ref[...])).astype(BF16)
    y = h + _dot(act, wd_ref[...])
    if final_norm:
        y = y * lax.rsqrt(jnp.mean(y * y, axis=-1, keepdims=True) + EPS) * gf_ref[...]
    o_ref[...] = y


def _ffn(h, gain, w_g, w_u, w_d, gain_final, final_norm):
    s, d = h.shape
    tm = TM_FFN
    once = lambda a: pl.BlockSpec(a.shape, lambda i: (0, 0), pipeline_mode=pl.Buffered(1))
    vec = pl.BlockSpec((1, d), lambda i: (0, 0))
    return pl.pallas_call(
        functools.partial(_ffn_kernel, final_norm=final_norm),
        out_shape=jax.ShapeDtypeStruct((s, d), F32),
        grid=(s // tm,),
        in_specs=[pl.BlockSpec((tm, d), lambda i: (i, 0)), vec, once(w_g), once(w_u), once(w_d), vec],
        out_specs=pl.BlockSpec((tm, d), lambda i: (i, 0)),
        compiler_params=pltpu.CompilerParams(
            dimension_semantics=("parallel",), vmem_limit_bytes=VMEM_LIMIT),
        name="ffn",
    )(h, gain.reshape(1, d), w_g, w_u, w_d, gain_final.reshape(1, d))


def _gate_weight_columns(w_in, col_beta, col_a):
    hps = DELTA_HEADS_PER_STEP
    groups = N_HEADS_B // hps
    d = w_in.shape[0]
    w = jnp.zeros((d, groups, LANES), w_in.dtype)
    w = w.at[:, :, 0:hps].set(w_in[:, col_beta:col_beta + N_HEADS_B].reshape(d, groups, hps))
    w = w.at[:, :, 8:8 + hps].set(w_in[:, col_a:col_a + N_HEADS_B].reshape(d, groups, hps))
    return w.reshape(d, groups * LANES)


def kernel(x, norm_mix, w_in, conv_w, a_log, dt_bias, dn_norm, w_proj_attn, w_proj_delta,
           w_out, norm_ffn, w_gate, w_up, w_down, norm_final):
    b, s, d = x.shape
    assert b == 1 and s % (TM_PROJ * 16) == 0 and d == 8 * LANES
    depth = w_in.shape[0]
    c_attn = 3 * D_ATTN
    c_z = c_attn + 4 * D_DELTA
    c_beta, c_a = c_z, c_z + N_HEADS_B
    c_gate = c_z + 2 * N_HEADS_B
    h = x.reshape(s, d)
    for layer in range(depth):
        wl = w_in[layer]
        c_dq = c_attn
        w_attn = [jnp.concatenate([wl[:, sec * D_ATTN + g * D_ATTN_OUT:sec * D_ATTN + (g + 1) * D_ATTN_OUT]
                                   for sec in range(3)], axis=1).astype(BF16) for g in range(len(ATTN_PATTERNS))]
        w_conv = wl[:, c_dq:c_dq + 3 * D_DELTA].astype(BF16)
        w_zg = jnp.concatenate([wl[:, c_dq + 3 * D_DELTA:c_z], wl[:, c_gate:]], axis=1).astype(BF16)
        w_ba = _gate_weight_columns(wl, c_beta, c_a).astype(BF16)
        *qkv_attn, qkv_d, zg, ba = _in_proj(h, norm_mix[layer], w_attn, w_conv, w_zg, w_ba, conv_w[layer])

        outs, lses = zip(*[_attn_group(qkv_g, g) for g, qkv_g in enumerate(qkv_attn)])
        o_delta = _delta_rule(qkv_d, zg, ba, a_log[layer], dt_bias[layer], dn_norm[layer])
        h = _merge_proj(outs, lses, o_delta, zg, h, w_proj_attn[layer].astype(BF16),
                        w_proj_delta[layer].astype(BF16), w_out[layer].astype(BF16))
        h = _ffn(h, norm_ffn[layer], w_gate[layer].astype(BF16), w_up[layer].astype(BF16),
                 w_down[layer].astype(BF16), norm_final, layer == depth - 1)
    return h.reshape(b, s, d)
```
